```python
import jax, jax.numpy as jnp
from jax import lax
import numpy as np

D_MODEL = 1024
BATCH = 8
SEQ = 2048
DEPTH = 1
DEC_BATCH = 32
DEC_SEQ = 32
PAST_LEN = 2048

CHUNK = 64
D_MIX = D_MODEL
RET_WIDTH = D_MIX // 2
RET_HEADS = 4
RET_DK = RET_WIDTH // RET_HEADS
RET_DV = RET_WIDTH // RET_HEADS
RET_ROPE_BASE = 10000.0
ATT_WIDTH = D_MIX - RET_WIDTH
ATT_HEADS = 8
ATT_DH = ATT_WIDTH // ATT_HEADS
ATT_ROT = ATT_DH // 4
IDX_HEADS = 8
IDX_DH = 64
IDX_ROT = IDX_DH // 4
TOPK_MAX = 256
ROPE_THETA = 500000.0
Q_BLOCK = 128
D_FF = 2816
LN_EPS = 1e-5
ALPHA = (2.0 * DEPTH) ** 0.25
BETA = (8.0 * DEPTH) ** -0.25
NEG = -1e30
IN_SIZES = (RET_WIDTH, RET_WIDTH, RET_HEADS * RET_DV, RET_HEADS * RET_DV,
            ATT_WIDTH, ATT_WIDTH, ATT_WIDTH, IDX_HEADS * IDX_DH, IDX_DH, IDX_HEADS)
IN_BETA = (1.0, 1.0, BETA, 1.0, 1.0, 1.0, BETA, 1.0, 1.0, 1.0)
D_IN = sum(IN_SIZES)

kernel_name = "hybrid_retention_dsa_streaming_encoder_step"

F32 = jnp.float32


def layer_norm(x, g, b):
    xf = x.astype(F32)
    mu = jnp.mean(xf, -1, keepdims=True)
    var = jnp.mean(jnp.square(xf - mu), -1, keepdims=True)
    return ((xf - mu) * lax.rsqrt(var + LN_EPS)).astype(x.dtype) * g + b


def head_norm(x):
    xf = x.astype(F32)
    mu = jnp.mean(xf, -1, keepdims=True)
    var = jnp.mean(jnp.square(xf - mu), -1, keepdims=True)
    return (xf - mu) * lax.rsqrt(var + LN_EPS)


def rope_angles(pos, rot_dim, base):
    inv = 1.0 / (base ** (jnp.arange(0, rot_dim, 2, dtype=F32) / rot_dim))
    ang = pos.astype(F32)[:, None] * inv[None, :]
    return jnp.cos(ang), jnp.sin(ang)


def apply_rope(x, cos, sin):
    r = 2 * cos.shape[-1]
    if x.ndim == 4:
        cos, sin = cos[:, None, :], sin[:, None, :]
    xf = x.astype(F32)
    x1, x2, xp = xf[..., : r // 2], xf[..., r // 2: r], xf[..., r:]
    return jnp.concatenate([x1 * cos - x2 * sin, x2 * cos + x1 * sin, xp], -1).astype(x.dtype)


def modulate(x, shift, scale):
    return x * (1.0 + scale[:, None, :]) + shift[:, None, :]


def post_norm(x, out, gate, weight, g, b):
    return layer_norm(ALPHA * x + weight * gate[:, None, :] * out, g, b)


def swiglu(h, w_gate, w_up, w_down):
    a = jnp.einsum('btd,df->btf', h, w_gate)
    u = jnp.einsum('btd,df->btf', h, w_up)
    return jnp.einsum('btf,fd->btd', jax.nn.silu(a) * u, w_down)


def mixer_inputs(h, pos, w_in):
    B, T, _ = h.shape
    z = jnp.einsum('btd,de->bte', h, w_in)
    split_points = [int(s) for s in np.cumsum(IN_SIZES)[:-1]]
    rq, rk, rv, rg, aq, ak, av, iq, ik, iw = jnp.split(z, split_points, axis=-1)
    rcos, rsin = rope_angles(pos, RET_DK, RET_ROPE_BASE)
    acos, asin = rope_angles(pos, ATT_ROT, ROPE_THETA)
    icos, isin = rope_angles(pos, IDX_ROT, ROPE_THETA)
    rq = apply_rope(rq.reshape(B, T, RET_HEADS, RET_DK), rcos, rsin)
    rk = apply_rope(rk.reshape(B, T, RET_HEADS, RET_DK), rcos, rsin) * (RET_DK ** -0.5)
    rv = rv.reshape(B, T, RET_HEADS, RET_DV)
    rg = rg.reshape(B, T, RET_HEADS, RET_DV)
    aq = apply_rope(aq.reshape(B, T, ATT_HEADS, ATT_DH), acos, asin)
    ak = apply_rope(ak.reshape(B, T, ATT_HEADS, ATT_DH), acos, asin)
    av = av.reshape(B, T, ATT_HEADS, ATT_DH)
    iq = apply_rope(iq.reshape(B, T, IDX_HEADS, IDX_DH), icos, isin)
    ik = apply_rope(ik, icos, isin)
    iw = iw * (IDX_HEADS ** -0.5)
    return rq, rk, rv, rg, aq, ak, av, iq, ik, iw


def retention_log_decay():
    return jnp.log(1.0 - 2.0 ** (-5.0 - jnp.arange(RET_HEADS, dtype=F32)))


def retention_chunk(S, q, k, v):
    lg = retention_log_decay()
    C = q.shape[1]
    n = jnp.arange(C, dtype=F32)
    rel = n[:, None] - n[None, :]
    D = jnp.where(rel[None] >= 0, jnp.exp(lg[:, None, None] * jnp.maximum(rel, 0.0)[None]), 0.0)
    qf, kf, vf = q.astype(F32), k.astype(F32), v.astype(F32)
    scores = jnp.einsum('bnhd,bmhd->bhnm', qf, kf) * D[None]
    inner = jnp.einsum('bhnm,bmhe->bnhe', scores, vf)
    cross = jnp.einsum('bnhd,bhde->bnhe', qf, S) * jnp.exp(lg[None, :] * (n[:, None] + 1.0))[None, :, :, None]
    decay_k = jnp.exp(lg[None, :] * (C - 1.0 - n)[:, None])
    S_new = jnp.exp(lg * C)[None, :, None, None] * S + jnp.einsum('bmhd,mh,bmhe->bhde', kf, decay_k, vf)
    return S_new, inner + cross


def retention_prompt(q, k, v):
    B, T, H, dk = q.shape
    nc = T // CHUNK

    def to_chunks(a):
        return a.reshape(B, nc, CHUNK, *a.shape[2:]).swapaxes(0, 1)

    S0 = jnp.zeros((B, H, dk, RET_DV), F32)
    S, o = lax.scan(lambda s, qkv: retention_chunk(s, *qkv), S0, (to_chunks(q), to_chunks(k), to_chunks(v)))
    return S, o.swapaxes(0, 1).reshape(B, T, H, RET_DV)


def dsa_attend(q, iq, iw, qpos, k, v, ik, topk):
    L = k.shape[1]
    s = jnp.einsum('bqhd,bsd->bqhs', iq.astype(F32), ik.astype(F32)) * (IDX_DH ** -0.5)
    index_score = jnp.einsum('bqh,bqhs->bqs', iw.astype(F32), jax.nn.relu(s))
    limit = (qpos // CHUNK + 1) * CHUNK
    admissible = jnp.arange(L, dtype=jnp.int32)[None, :] < limit[:, None]
    index_score = jnp.where(admissible[None], index_score, NEG)
    _, idx = lax.top_k(index_score, topk)
    valid = idx < limit[None, :, None]
    kg = jax.vmap(lambda kk, ii: kk[ii])(k, idx)
    vg = jax.vmap(lambda vv, ii: vv[ii])(v, idx)
    logits = jnp.einsum('bqhd,bqkhd->bqhk', q.astype(F32), kg.astype(F32)) * (ATT_DH ** -0.5)
    logits = jnp.where(valid[:, :, None, :], logits, NEG)
    p = jax.nn.softmax(logits, axis=-1)
    return jnp.einsum('bqhk,bqkhd->bqhd', p, vg.astype(F32)).astype(q.dtype)


def dsa_prompt(q, k, v, iq, ik, iw):
    B, T = q.shape[:2]
    nb = T // Q_BLOCK
    topk = min(TOPK_MAX, T // 4)

    def blocks(a):
        return a.reshape(B, nb, Q_BLOCK, *a.shape[2:]).swapaxes(0, 1)

    qpos = jnp.arange(T, dtype=jnp.int32).reshape(nb, Q_BLOCK)
    o = lax.map(lambda a: dsa_attend(a[0], a[1], a[2], a[3], k, v, ik, topk),
                (blocks(q), blocks(iq), blocks(iw), qpos))
    return o.swapaxes(0, 1).reshape(B, T, ATT_HEADS, ATT_DH)


def encoder_layer(x, c, w_cond, b_cond, f1_gate, f1_up, f1_down, ln1_g, ln1_b, w_in, w_out, ln2_g, ln2_b,
                  f2_gate, f2_up, f2_down, ln3_g, ln3_b, past):
    B, T, _ = x.shape
    mod = jnp.einsum('bd,de->be', jax.nn.silu(c), w_cond) + b_cond
    sh1, sc1, gt1, sh2, sc2, gt2, sh3, sc3, gt3 = jnp.split(mod, 9, axis=-1)
    x = post_norm(x, swiglu(modulate(x, sh1, sc1), f1_gate, f1_up, f1_down), 1.0 + gt1, 0.5, ln1_g, ln1_b)
    h = modulate(x, sh2, sc2)
    if past is None:
        pos = jnp.arange(T, dtype=jnp.int32)
    else:
        past_k, past_v, past_ik, ret_s0 = past
        pos = past_k.shape[1] + jnp.arange(T, dtype=jnp.int32)
    rq, rk, rv, rg, aq, ak, av, iq, ik, iw = mixer_inputs(h, pos, w_in)
    if past is None:
        ret_s, o_ret = retention_prompt(rq, rk, rv)
        o_att = dsa_prompt(aq, ak, av, iq, ik, iw)
    else:
        ret_s, o_ret = retention_chunk(ret_s0.astype(F32), rq, rk, rv)
        k_all = jnp.concatenate([past_k, ak], axis=1)
        v_all = jnp.concatenate([past_v, av], axis=1)
        ik_all = jnp.concatenate([past_ik, ik], axis=1)
        L = k_all.shape[1]
        o_att = dsa_attend(aq, iq, iw, pos, k_all, v_all, ik_all, min(TOPK_MAX, L // 4))
    o_ret = (head_norm(o_ret) * jax.nn.silu(rg.astype(F32))).astype(x.dtype).reshape(B, T, RET_WIDTH)
    mixed = jnp.concatenate([o_ret, o_att.reshape(B, T, ATT_WIDTH)], axis=-1)
    x = post_norm(x, jnp.einsum('btm,md->btd', mixed, w_out), 1.0 + gt2, 1.0, ln2_g, ln2_b)
    x = post_norm(x, swiglu(modulate(x, sh3, sc3), f2_gate, f2_up, f2_down), 1.0 + gt3, 0.5, ln3_g, ln3_b)
    return x, (ak, av, ik, ret_s.astype(x.dtype))


def setup_inputs(seed: int = 0) -> dict:
    key = jax.random.key(seed)
    ks = jax.random.split(key, 32)
    nrm = jax.random.normal
    d_in_scale = jnp.asarray(np.concatenate([np.full((s,), b, np.float32) for s, b in zip(IN_SIZES, IN_BETA)]))
    return {
        'x_prompt': nrm(ks[0], (BATCH, SEQ, D_MODEL), F32),
        'x_sample': nrm(ks[1], (DEC_BATCH, DEC_SEQ, D_MODEL), F32),
        'c_prompt': nrm(ks[2], (BATCH, D_MODEL), F32),
        'c_sample': nrm(ks[3], (DEC_BATCH, D_MODEL), F32),
        'cache_k': nrm(ks[4], (DEPTH, DEC_BATCH, PAST_LEN, ATT_HEADS, ATT_DH), F32),
        'cache_v': nrm(ks[5], (DEPTH, DEC_BATCH, PAST_LEN, ATT_HEADS, ATT_DH), F32),
        'cache_idx_k': nrm(ks[6], (DEPTH, DEC_BATCH, PAST_LEN, IDX_DH), F32),
        'state_ret': nrm(ks[7], (DEPTH, DEC_BATCH, RET_HEADS, RET_DK, RET_DV), F32),
        'w_cond': nrm(ks[8], (DEPTH, D_MODEL, 9 * D_MODEL), F32) * (0.1 * D_MODEL ** -0.5),
        'b_cond': nrm(ks[9], (DEPTH, 9 * D_MODEL), F32) * 0.01,
        'ffn1_w_gate': nrm(ks[10], (DEPTH, D_MODEL, D_FF), F32) * D_MODEL ** -0.5,
        'ffn1_w_up': nrm(ks[11], (DEPTH, D_MODEL, D_FF), F32) * D_MODEL ** -0.5,
        'ffn1_w_down': nrm(ks[12], (DEPTH, D_FF, D_MODEL), F32) * (BETA * D_FF ** -0.5),
        'ln1_g': 1.0 + 0.02 * nrm(ks[13], (DEPTH, D_MODEL), F32),
        'ln1_b': 0.02 * nrm(ks[14], (DEPTH, D_MODEL), F32),
        'w_in': nrm(ks[15], (DEPTH, D_MODEL, D_IN), F32) * (D_MODEL ** -0.5) * d_in_scale,
        'w_out': nrm(ks[16], (DEPTH, D_MIX, D_MODEL), F32) * (BETA * D_MIX ** -0.5),
        'ln2_g': 1.0 + 0.02 * nrm(ks[17], (DEPTH, D_MODEL), F32),
        'ln2_b': 0.02 * nrm(ks[18], (DEPTH, D_MODEL), F32),
        'ffn2_w_gate': nrm(ks[19], (DEPTH, D_MODEL, D_FF), F32) * D_MODEL ** -0.5,
        'ffn2_w_up': nrm(ks[20], (DEPTH, D_MODEL, D_FF), F32) * D_MODEL ** -0.5,
        'ffn2_w_down': nrm(ks[21], (DEPTH, D_FF, D_MODEL), F32) * (BETA * D_FF ** -0.5),
        'ln3_g': 1.0 + 0.02 * nrm(ks[22], (DEPTH, D_MODEL), F32),
        'ln3_b': 0.02 * nrm(ks[23], (DEPTH, D_MODEL), F32),
    }


def reference(x_prompt, x_sample, c_prompt, c_sample, cache_k, cache_v, cache_idx_k, state_ret,
              w_cond, b_cond, ffn1_w_gate, ffn1_w_up, ffn1_w_down, ln1_g, ln1_b, w_in, w_out, ln2_g, ln2_b,
              ffn2_w_gate, ffn2_w_up, ffn2_w_down, ln3_g, ln3_b):
    y_prompt, y_sample = x_prompt, x_sample
    st_p, st_s = [], []
    for l in range(DEPTH):
        params = (w_cond[l], b_cond[l], ffn1_w_gate[l], ffn1_w_up[l], ffn1_w_down[l], ln1_g[l], ln1_b[l],
                  w_in[l], w_out[l], ln2_g[l], ln2_b[l], ffn2_w_gate[l], ffn2_w_up[l], ffn2_w_down[l],
                  ln3_g[l], ln3_b[l])
        y_prompt, sp = encoder_layer(y_prompt, c_prompt, *params, past=None)
        y_sample, ss = encoder_layer(y_sample, c_sample, *params,
                                     past=(cache_k[l], cache_v[l], cache_idx_k[l], state_ret[l]))
        st_p.append(sp)
        st_s.append(ss)
    new_k_prompt = jnp.stack([s[0] for s in st_p])
    new_v_prompt = jnp.stack([s[1] for s in st_p])
    new_idx_k_prompt = jnp.stack([s[2] for s in st_p])
    state_ret_prompt = jnp.stack([s[3] for s in st_p])
    new_k_sample = jnp.stack([s[0] for s in st_s])
    new_v_sample = jnp.stack([s[1] for s in st_s])
    new_idx_k_sample = jnp.stack([s[2] for s in st_s])
    state_ret_sample = jnp.stack([s[3] for s in st_s])
    return (y_prompt, y_sample, new_k_prompt, new_v_prompt, new_idx_k_prompt, state_ret_prompt,
            new_k_sample, new_v_sample, new_idx_k_sample, state_ret_sample)
```

```python
import functools
import math

import jax
import jax.numpy as jnp
import numpy as np
from jax import lax
from jax.experimental import pallas as pl
from jax.experimental.pallas import tpu as pltpu

F32 = jnp.float32
BF16 = jnp.bfloat16
I32 = jnp.int32

RET_HEADS = 4
RET_DK = 128
RET_ROPE_BASE = 10000.0
ATT_HEADS = 8
ATT_DH = 64
ATT_ROT = 16
IDX_HEADS = 8
ROPE_THETA = 500000.0
MASK_CHUNK = 64
TOPK_MAX = 256
LN_EPS = 1e-5
NEG = -1e30
INT_MIN = -2 ** 31
HALF = 512
LANES = 128
KEY_CHUNK = 256
VMEM_LIMIT = 56 * 1024 * 1024

_NT = (((1,), (1,)), ((), ()))
_TN = (((0,), (0,)), ((), ()))


def _params(sem):
    return pltpu.CompilerParams(dimension_semantics=sem, vmem_limit_bytes=VMEM_LIMIT)


def _layer_norm(v, g, b):
    mu = jnp.mean(v, -1, keepdims=True)
    d = v - mu
    var = jnp.mean(d * d, -1, keepdims=True)
    return d * lax.rsqrt(var + LN_EPS) * g + b


def _silu(a):
    return a * jax.nn.sigmoid(a)


def _resident(shape):
    nd = len(shape)
    return pl.BlockSpec(shape, lambda *_: (0,) * nd, pipeline_mode=pl.Buffered(1))


def _cond_kernel(c_ref, w_ref, b_ref, o_ref):
    s = _silu(c_ref[...])
    o_ref[...] = jnp.dot(s, w_ref[...], preferred_element_type=F32,
                         precision=lax.Precision.HIGHEST) + b_ref[...]


def _cond_call(c, w, b):
    n_rows, d = c.shape
    n = w.shape[1]
    tn = 1152
    return pl.pallas_call(
        _cond_kernel,
        grid=(n // tn,),
        in_specs=[pl.BlockSpec((n_rows, d), lambda i: (0, 0)),
                  pl.BlockSpec((d, tn), lambda i: (0, i)),
                  pl.BlockSpec((1, tn), lambda i: (0, i))],
        out_specs=pl.BlockSpec((n_rows, tn), lambda i: (0, i)),
        out_shape=jax.ShapeDtypeStruct((n_rows, n), F32),
        compiler_params=_params(("parallel",)),
    )(c, w, b.reshape(1, n))


def _block_kernel(*refs, alpha, with_mix, mod_idx, n_chunks):
    if with_mix:
        (x_ref, mod_ref, oret_ref, oatt_ref, wout_ref, g2_ref, b2_ref,
         wg_ref, wu_ref, wd_ref, g_ref, b_ref, o_ref) = refs
    else:
        x_ref, mod_ref, wg_ref, wu_ref, wd_ref, g_ref, b_ref, o_ref = refs
    x = x_ref[...]
    bb, tt, d = x.shape
    rows = bb * tt
    mod = mod_ref[...]
    if with_mix:
        mixed = (jnp.dot(oret_ref[...].reshape(rows, HALF), wout_ref[:HALF, :], preferred_element_type=F32)
                 + jnp.dot(oatt_ref[...].reshape(rows, HALF), wout_ref[HALF:, :], preferred_element_type=F32))
        x = _layer_norm(alpha * x + (1.0 + mod[:, 5:6, :]) * mixed.reshape(bb, tt, d), g2_ref[...], b2_ref[...])
    i_sh, i_sc, i_gt = mod_idx
    h = (x * (1.0 + mod[:, i_sc:i_sc + 1, :]) + mod[:, i_sh:i_sh + 1, :]).reshape(rows, d).astype(BF16)
    dff = wg_ref.shape[1]
    ck = dff // n_chunks
    f = None
    for i in range(n_chunks):
        a = jnp.dot(h, wg_ref[:, i * ck:(i + 1) * ck], preferred_element_type=F32)
        u = jnp.dot(h, wu_ref[:, i * ck:(i + 1) * ck], preferred_element_type=F32)
        act = (_silu(a) * u).astype(BF16)
        part = jnp.dot(act, wd_ref[i * ck:(i + 1) * ck, :], preferred_element_type=F32)
        f = part if f is None else f + part
    y = alpha * x + (0.5 * (1.0 + mod[:, i_gt:i_gt + 1, :])) * f.reshape(bb, tt, d)
    o_ref[...] = _layer_norm(y, g_ref[...], b_ref[...])


def _block_call(x, mod, wg, wu, wd, g, b, *, alpha, mod_idx, bb, tt, mix=None):
    bg, tg, d = x.shape
    dff = wg.shape[1]
    row_spec = lambda w: pl.BlockSpec((bb, tt, w), lambda i, j: (i, j, 0))
    mod_spec = pl.BlockSpec((bb, 9, d), lambda i, j: (i, 0, 0))
    args = [x, mod]
    specs = [row_spec(d), mod_spec]
    if mix is not None:
        oret, oatt, wout, g2, b2 = mix
        args += [oret, oatt, wout, g2.reshape(1, d), b2.reshape(1, d)]
        specs += [row_spec(HALF), row_spec(HALF), _resident((d, d)), _resident((1, d)), _resident((1, d))]
    args += [wg, wu, wd, g.reshape(1, d), b.reshape(1, d)]
    specs += [_resident((d, dff)), _resident((d, dff)), _resident((dff, d)), _resident((1, d)), _resident((1, d))]
    return pl.pallas_call(
        functools.partial(_block_kernel, alpha=alpha, with_mix=mix is not None, mod_idx=mod_idx, n_chunks=2),
        grid=(bg // bb, tg // tt),
        in_specs=specs,
        out_specs=row_spec(d),
        out_shape=jax.ShapeDtypeStruct((bg, tg, d), F32),
        compiler_params=_params(("parallel", "parallel")),
    )(*args)


def _rope_tables(pos):
    posf = pos.astype(F32)[:, None]
    inv_r = 1.0 / (RET_ROPE_BASE ** (jnp.arange(0, RET_DK, 2, dtype=F32) / RET_DK))
    ang = posf * inv_r[None, :]
    cos_r, sin_r = jnp.cos(ang), jnp.sin(ang)
    cr = jnp.concatenate([cos_r, cos_r], -1)
    sr = jnp.concatenate([-sin_r, sin_r], -1)
    kscale = RET_DK ** -0.5
    inv_a = 1.0 / (ROPE_THETA ** (jnp.arange(0, ATT_ROT, 2, dtype=F32) / ATT_ROT))
    ang = posf * inv_a[None, :]
    cos_a, sin_a = jnp.cos(ang), jnp.sin(ang)
    t = pos.shape[0]
    half = ATT_ROT // 2
    rest = ATT_DH - ATT_ROT
    c64 = jnp.concatenate([cos_a, cos_a, jnp.ones((t, rest), F32)], -1)
    s1 = jnp.concatenate([-sin_a, jnp.zeros((t, ATT_DH - half), F32)], -1)
    s2 = jnp.concatenate([jnp.zeros((t, half), F32), sin_a, jnp.zeros((t, rest), F32)], -1)
    ca, s1a, s2a = (jnp.concatenate([v, v], -1) for v in (c64, s1, s2))
    qscale = ATT_DH ** -0.5
    return jnp.stack([cr, sr, cr * kscale, sr * kscale,
                      ca * qscale, s1a * qscale, s2a * qscale, ca, s1a, s2a])


def _mixer_in_kernel(*refs, prompt_layout):
    x_ref, mod_ref, w_ref, tab_ref = refs[:4]
    outs = refs[4:]
    rq_ref, rk_ref, rv_ref, rg_ref, aq_ref, ak_ref, av_ref, iq_ref, ik2_ref, iw_ref = outs[:10]
    x = x_ref[...]
    bb, tt, d = x.shape
    rows = bb * tt
    mod = mod_ref[...]
    h = (x * (1.0 + mod[:, 4:5, :]) + mod[:, 3:4, :]).reshape(rows, d).astype(BF16)

    def seg(i, width=HALF):
        return jnp.dot(h, w_ref[:, HALF * i:HALF * i + width], preferred_element_type=F32)

    def tab(i):
        return tab_ref[i][None]

    def rope_ret_col(zc, ci, si):
        r = pltpu.roll(zc, 64, 1)
        return zc.reshape(bb, tt, LANES) * tab(ci) + r.reshape(bb, tt, LANES) * tab(si)

    def rope_att_col(zc, ci):
        r1 = pltpu.roll(zc, LANES - ATT_ROT // 2, 1)
        r2 = pltpu.roll(zc, ATT_ROT // 2, 1)
        return (zc.reshape(bb, tt, LANES) * tab(ci) + r1.reshape(bb, tt, LANES) * tab(ci + 1)
                + r2.reshape(bb, tt, LANES) * tab(ci + 2))

    def cols(z, fn):
        return jnp.concatenate([fn(z[:, LANES * g:LANES * (g + 1)]) for g in range(z.shape[1] // LANES)], axis=-1)

    rq_ref[...] = cols(seg(0), lambda zc: rope_ret_col(zc, 0, 1)).astype(BF16)
    rk_ref[...] = cols(seg(1), lambda zc: rope_ret_col(zc, 2, 3)).astype(BF16)
    rv_ref[...] = seg(2).reshape(bb, tt, HALF).astype(BF16)
    rg_ref[...] = seg(3).reshape(bb, tt, HALF).astype(BF16)
    aq_ref[...] = cols(seg(4), lambda zc: rope_att_col(zc, 4)).astype(BF16)
    ak_ref[...] = cols(seg(5), lambda zc: rope_att_col(zc, 7))
    av = seg(6)
    av_ref[...] = av.reshape(bb, tt, HALF)
    iq_ref[...] = cols(seg(7), lambda zc: rope_att_col(zc, 4)).astype(BF16)
    tail = seg(8, 2 * LANES)
    ik2_ref[...] = rope_att_col(tail[:, :LANES], 7)
    iw = tail[:, LANES:] * (IDX_HEADS ** -0.5)
    if prompt_layout:
        avt_ref = outs[10]
        iw_ref[0] = iw.T[:IDX_HEADS, :]
        avt_ref[0, 0] = av.T.astype(BF16)
    else:
        iw_ref[...] = iw.reshape(bb, tt, LANES)


def _mixer_in_call(x, mod, w_in_p, tabs, *, bb, tt, prompt_layout):
    bg, tg, d = x.shape
    row = lambda w: pl.BlockSpec((bb, tt, w), lambda i, j: (i, j, 0))
    sds = lambda w, dt: jax.ShapeDtypeStruct((bg, tg, w), dt)
    out_specs = [row(HALF)] * 8 + [row(LANES)]
    out_shape = [sds(HALF, BF16), sds(HALF, BF16), sds(HALF, BF16), sds(HALF, BF16), sds(HALF, BF16),
                 sds(HALF, F32), sds(HALF, F32), sds(HALF, BF16), sds(LANES, F32)]
    if prompt_layout:
        assert bb == 1 and tt == KEY_CHUNK
        out_specs += [pl.BlockSpec((1, IDX_HEADS, tt), lambda i, j: (i, 0, j)),
                      pl.BlockSpec((1, 1, HALF, tt), lambda i, j: (i, j, 0, 0))]
        out_shape += [jax.ShapeDtypeStruct((bg, IDX_HEADS, tg), F32),
                      jax.ShapeDtypeStruct((bg, tg // tt, HALF, tt), BF16)]
    else:
        out_specs += [row(LANES)]
        out_shape += [sds(LANES, F32)]
    return pl.pallas_call(
        functools.partial(_mixer_in_kernel, prompt_layout=prompt_layout),
        grid=(bg // bb, tg // tt),
        in_specs=[row(d), pl.BlockSpec((bb, 9, d), lambda i, j: (i, 0, 0)),
                  _resident(w_in_p.shape),
                  pl.BlockSpec((10, tt, LANES), lambda i, j: (0, j, 0))],
        out_specs=out_specs,
        out_shape=out_shape,
        compiler_params=_params(("parallel", "parallel")),
    )(x, mod, w_in_p, tabs)


def _retention_tables(c):
    lg = jnp.log(1.0 - 2.0 ** (-5.0 - jnp.arange(RET_HEADS, dtype=F32)))
    n = jnp.arange(c, dtype=F32)
    rel = n[:, None] - n[None, :]
    dmat = jnp.where(rel[None] >= 0, jnp.exp(lg[:, None, None] * jnp.maximum(rel, 0.0)[None]), 0.0)
    cross = jnp.exp(lg[:, None] * (n[None, :] + 1.0))
    kdec = jnp.exp(lg[:, None] * (c - 1.0 - n)[None, :])
    sdec = jnp.exp(lg * c)
    bc = lambda v: jnp.broadcast_to(v[:, :, None], (RET_HEADS, c, LANES))
    return dmat, bc(cross), bc(kdec), jnp.broadcast_to(sdec[:, None, None], (RET_HEADS, 8, LANES))


def _retention_kernel(rq_ref, rk_ref, rv_ref, rg_ref, s0_ref, dmat_ref, cdec_ref, kdec_ref, sdec_ref,
                      o_ref, sout_ref, s_scr):
    @pl.when(pl.program_id(1) == 0)
    def _():
        s_scr[...] = s0_ref[0]

    for hd in range(RET_HEADS):
        sl = slice(RET_DK * hd, RET_DK * (hd + 1))
        q = rq_ref[0, :, sl]
        k = rk_ref[0, :, sl]
        v = rv_ref[0, :, sl]
        state = s_scr[hd]
        scores = lax.dot_general(q, k, _NT, preferred_element_type=F32) * dmat_ref[hd]
        inner = jnp.dot(scores.astype(BF16), v, preferred_element_type=F32)
        cross = jnp.dot(q, state.astype(BF16), preferred_element_type=F32) * cdec_ref[hd]
        kd = (k.astype(F32) * kdec_ref[hd]).astype(BF16)
        s_new = sdec_ref[hd, 0:1, :] * state + lax.dot_general(kd, v, _TN, preferred_element_type=F32)
        s_scr[hd] = s_new
        o = inner + cross
        mu = jnp.mean(o, -1, keepdims=True)
        dlt = o - mu
        var = jnp.mean(dlt * dlt, -1, keepdims=True)
        o_ref[0, :, sl] = (dlt * lax.rsqrt(var + LN_EPS) * _silu(rg_ref[0, :, sl].astype(F32))).astype(BF16)
    sout_ref[0] = s_scr[...]


def _retention_call(rq, rk, rv, rg, s0, *, chunk):
    bg, tg, _ = rq.shape
    dmat, cdec, kdec, sdec = _retention_tables(chunk)
    row = pl.BlockSpec((1, chunk, HALF), lambda b, c: (b, c, 0))
    st = pl.BlockSpec((1, RET_HEADS, RET_DK, RET_DK), lambda b, c: (b, 0, 0, 0))
    return pl.pallas_call(
        _retention_kernel,
        grid=(bg, tg // chunk),
        in_specs=[row, row, row, row, st, _resident(dmat.shape), _resident(cdec.shape), _resident(kdec.shape),
                  _resident(sdec.shape)],
        out_specs=[row, st],
        out_shape=[jax.ShapeDtypeStruct((bg, tg, HALF), BF16),
                   jax.ShapeDtypeStruct((bg, RET_HEADS, RET_DK, RET_DK), F32)],
        scratch_shapes=[pltpu.VMEM((RET_HEADS, RET_DK, RET_DK), F32)],
        compiler_params=_params(("parallel", "arbitrary")),
    )(rq, rk, rv, rg, s0, dmat, cdec, kdec, sdec)


def _sortable_key(score):
    bits = pltpu.bitcast(score, I32)
    return bits ^ ((bits >> 31) & 0x7FFFFFFF)


def _kth_largest_key(count_ge, topk, like):
    lo = jnp.where(count_ge(jnp.zeros_like(like)) >= topk, 0, INT_MIN).astype(I32)

    def body(i, lo):
        cand = lo + jnp.left_shift(jnp.int32(1), 30 - i)
        return jnp.where(count_ge(cand) >= topk, cand, lo)

    return lax.fori_loop(0, 31, body, lo)


def _tie_cut(count_tie_below, need, n_pos_bits, like):
    def body(i, j0):
        cand = j0 + jnp.left_shift(jnp.int32(1), n_pos_bits - 1 - i)
        return jnp.where(count_tie_below(cand) < need, cand, j0)

    return lax.fori_loop(0, n_pos_bits, body, jnp.zeros_like(like)) + 1


def _dsa_prompt_kernel(iq_ref, iwt_ref, aq_ref, ik2_ref, ak_ref, avt_ref, o_ref,
                       keys_scr, bias_scr, iqm_scr, aqm_scr, outt_scr, *, topk, seq_len):
    qb = kc = KEY_CHUNK
    j = pl.program_id(1)
    nch = j + 1
    lane_q = lax.broadcasted_iota(I32, (1, qb), 1)
    limit = ((j * qb + lane_q) // MASK_CHUNK + 1) * MASK_CHUNK
    row_pos = lax.broadcasted_iota(I32, (kc, qb), 0)

    lane_head = lax.broadcasted_iota(I32, (qb, LANES), 1) // ATT_DH
    for hd in range(ATT_HEADS):
        col = slice(LANES * (hd // 2), LANES * (hd // 2 + 1))
        keep = lane_head == (hd % 2)
        iqm_scr[hd] = jnp.where(keep, iq_ref[0, :, col], jnp.zeros((), BF16))
        aqm_scr[hd] = jnp.where(keep, aq_ref[0, :, col], jnp.zeros((), BF16))

    iwt = iwt_ref[0]

    def index_chunk(c, carry):
        off = pl.multiple_of(c * kc, kc)
        ikc = ik2_ref[0, pl.ds(off, kc), :].astype(BF16)
        acc = jnp.zeros((kc, qb), F32)
        for hd in range(IDX_HEADS):
            s = lax.dot_general(ikc, iqm_scr[hd], _NT, preferred_element_type=F32)
            acc = acc + iwt[hd:hd + 1, :] * jnp.maximum(s, 0.0)
        keys_scr[c] = jnp.where(row_pos + off < limit, _sortable_key(acc), INT_MIN)
        return carry

    lax.fori_loop(0, nch, index_chunk, 0)

    def count(pred):
        def body(c, acc):
            m = pred(keys_scr[c], row_pos + c * kc)
            return acc + jnp.sum(m.reshape(kc // 8, 8, qb), axis=0)
        acc = lax.fori_loop(0, nch, body, jnp.zeros((8, qb), I32))
        return jnp.sum(acc, axis=0, keepdims=True)

    one, zero = jnp.int32(1), jnp.int32(0)
    thr = _kth_largest_key(lambda t: count(lambda k, p: jnp.where(k >= t, one, zero)), topk, lane_q)
    n_gt = count(lambda k, p: jnp.where(k > thr, one, zero))
    n_eq = count(lambda k, p: jnp.where(k == thr, one, zero))
    need = topk - n_gt
    excess = jnp.where(thr == INT_MIN, 0, n_eq - need)
    take_all = jnp.where(thr == INT_MIN, 0, seq_len + 1).astype(I32)

    def tie_break():
        below = lambda jc: count(lambda k, p: jnp.where(k == thr, jnp.where(p < jc, one, zero), zero))
        cut = _tie_cut(below, need, max(1, int(math.ceil(math.log2(seq_len)))), lane_q)
        return jnp.where(excess > 0, cut, take_all)

    cut = lax.cond(jnp.max(excess) > 0, tie_break, lambda: take_all)

    def bias_chunk(c, carry):
        k = keys_scr[c]
        pos = row_pos + c * kc
        bias_scr[c] = jnp.where(k > thr, 0.0, jnp.where(k == thr, jnp.where(pos < cut, 0.0, NEG), NEG))
        return carry

    lax.fori_loop(0, nch, bias_chunk, 0)

    for hd in range(ATT_HEADS):
        col = slice(LANES * (hd // 2), LANES * (hd // 2 + 1))

        def attend(c, carry, hd=hd, col=col):
            m, l, acc = carry
            off = pl.multiple_of(c * kc, kc)
            kcol = ak_ref[0, pl.ds(off, kc), col].astype(BF16)
            lg = lax.dot_general(kcol, aqm_scr[hd], _NT, preferred_element_type=F32) + bias_scr[c]
            m_new = jnp.maximum(m, jnp.max(lg, axis=0, keepdims=True))
            alpha = jnp.exp(m - m_new)
            p = jnp.exp(lg - m_new)
            l = alpha * l + jnp.sum(p, axis=0, keepdims=True)
            vt = avt_ref[0, c, ATT_DH * hd:ATT_DH * (hd + 1), :]
            acc = alpha * acc + jnp.dot(vt, p.astype(BF16), preferred_element_type=F32)
            return m_new, l, acc

        init = (jnp.full((1, qb), NEG, F32), jnp.zeros((1, qb), F32), jnp.zeros((ATT_DH, qb), F32))
        _, l, acc = lax.fori_loop(0, nch, attend, init)
        outt_scr[ATT_DH * hd:ATT_DH * (hd + 1), :] = acc / l
    o_ref[0] = outt_scr[...].T.astype(BF16)


def _dsa_prompt_call(iq, iwt, aq, ik2, ak, avt):
    bg, tg, _ = iq.shape
    qb = KEY_CHUNK
    nc = tg // qb
    topk = min(TOPK_MAX, tg // 4)
    assert topk <= qb
    blk = pl.BlockSpec((1, qb, HALF), lambda b, j: (b, j, 0))
    return pl.pallas_call(
        functools.partial(_dsa_prompt_kernel, topk=topk, seq_len=tg),
        grid=(bg, nc),
        in_specs=[blk,
                  pl.BlockSpec((1, IDX_HEADS, qb), lambda b, j: (b, 0, j)),
                  blk,
                  pl.BlockSpec((1, tg, LANES), lambda b, j: (b, 0, 0)),
                  pl.BlockSpec((1, tg, HALF), lambda b, j: (b, 0, 0)),
                  pl.BlockSpec((1, nc, HALF, qb), lambda b, j: (b, 0, 0, 0))],
        out_specs=blk,
        out_shape=jax.ShapeDtypeStruct((bg, tg, HALF), BF16),
        scratch_shapes=[pltpu.VMEM((nc, qb, qb), I32), pltpu.VMEM((nc, qb, qb), F32),
                        pltpu.VMEM((ATT_HEADS, qb, LANES), BF16), pltpu.VMEM((ATT_HEADS, qb, LANES), BF16),
                        pltpu.VMEM((HALF, qb), F32)],
        compiler_params=_params(("parallel", "arbitrary")),
    )(iq, iwt, aq, ik2, ak, avt)


def _dsa_sample_kernel(iq_ref, iw_ref, aq_ref, ik2n_ref, akn_ref, avn_ref, cik_ref, ck_ref, cv_ref, o_ref,
                       *, topk, past_len):
    nq = iq_ref.shape[1]
    n_new = nq
    total = past_len + n_new
    iq = iq_ref[0]
    aq = aq_ref[0]
    iw = iw_ref[0]

    iq_rows = jnp.concatenate([iq[:, ATT_DH * hd:ATT_DH * (hd + 1)] for hd in range(IDX_HEADS)], axis=0)
    ik_c = cik_ref[0].astype(BF16)
    ik_n = ik2n_ref[0][:, :ATT_DH].astype(BF16)

    def index_scores(ik):
        s = lax.dot_general(iq_rows, ik, _NT, preferred_element_type=F32)
        acc = jnp.zeros((nq, ik.shape[0]), F32)
        for hd in range(IDX_HEADS):
            acc = acc + iw[:, hd:hd + 1] * jnp.maximum(s[nq * hd:nq * (hd + 1), :], 0.0)
        return acc

    qpos = past_len + lax.broadcasted_iota(I32, (nq, 1), 0)
    limit = (qpos // MASK_CHUNK + 1) * MASK_CHUNK
    pos_c = lax.broadcasted_iota(I32, (nq, past_len), 1)
    pos_n = past_len + lax.broadcasted_iota(I32, (nq, n_new), 1)
    key_c = jnp.where(pos_c < limit, _sortable_key(index_scores(ik_c)), INT_MIN)
    key_n = jnp.where(pos_n < limit, _sortable_key(index_scores(ik_n)), INT_MIN)

    def count(pred):
        return (jnp.sum(pred(key_c, pos_c), axis=1, keepdims=True)
                + jnp.sum(pred(key_n, pos_n), axis=1, keepdims=True))

    one, zero = jnp.int32(1), jnp.int32(0)
    thr = _kth_largest_key(lambda t: count(lambda k, p: jnp.where(k >= t, one, zero)), topk, qpos)
    n_gt = count(lambda k, p: jnp.where(k > thr, one, zero))
    n_eq = count(lambda k, p: jnp.where(k == thr, one, zero))
    need = topk - n_gt
    excess = jnp.where(thr == INT_MIN, 0, n_eq - need)
    take_all = jnp.where(thr == INT_MIN, 0, total + 1).astype(I32)

    def tie_break():
        below = lambda jc: count(lambda k, p: jnp.where(k == thr, jnp.where(p < jc, one, zero), zero))
        cut = _tie_cut(below, need, max(1, int(math.ceil(math.log2(total)))), qpos)
        return jnp.where(excess > 0, cut, take_all)

    cut = lax.cond(jnp.max(excess) > 0, tie_break, lambda: take_all)

    def bias(k, pos):
        return jnp.where(k > thr, 0.0, jnp.where(k == thr, jnp.where(pos < cut, 0.0, NEG), NEG))

    bias_c, bias_n = bias(key_c, pos_c), bias(key_n, pos_n)

    lane_head = lax.broadcasted_iota(I32, (nq, HALF), 1) // ATT_DH
    q_bd = jnp.concatenate([jnp.where(lane_head == hd, aq, jnp.zeros((), BF16)) for hd in range(ATT_HEADS)], axis=0)
    rows = ATT_HEADS * nq

    def logits(k, b):
        lg = lax.dot_general(q_bd, k.astype(BF16), _NT, preferred_element_type=F32)
        return (lg.reshape(ATT_HEADS, nq, k.shape[0]) + b[None]).reshape(rows, k.shape[0])

    lg_c = logits(ck_ref[0], bias_c)
    lg_n = logits(akn_ref[0], bias_n)
    m = jnp.maximum(jnp.max(lg_c, axis=1, keepdims=True), jnp.max(lg_n, axis=1, keepdims=True))
    p_c = jnp.exp(lg_c - m)
    p_n = jnp.exp(lg_n - m)
    l = jnp.sum(p_c, axis=1, keepdims=True) + jnp.sum(p_n, axis=1, keepdims=True)
    res = (jnp.dot(p_c.astype(BF16), cv_ref[0].astype(BF16), preferred_element_type=F32)
           + jnp.dot(p_n.astype(BF16), avn_ref[0].astype(BF16), preferred_element_type=F32)) / l
    res = res.reshape(ATT_HEADS, nq, HALF)
    out = jnp.zeros((nq, HALF), F32)
    for hd in range(ATT_HEADS):
        out = out + jnp.where(lane_head == hd, res[hd], 0.0)
    o_ref[0] = out.astype(BF16)


def _dsa_sample_call(iq, iw, aq, ik2, ak, av, cache_ik, cache_k, cache_v):
    bg, nq, _ = iq.shape
    past_len = cache_k.shape[1]
    topk = min(TOPK_MAX, (past_len + nq) // 4)
    per_b = lambda n, w: pl.BlockSpec((1, n, w), lambda b: (b, 0, 0))
    return pl.pallas_call(
        functools.partial(_dsa_sample_kernel, topk=topk, past_len=past_len),
        grid=(bg,),
        in_specs=[per_b(nq, HALF), per_b(nq, LANES), per_b(nq, HALF), per_b(nq, LANES), per_b(nq, HALF),
                  per_b(nq, HALF), per_b(past_len, ATT_DH), per_b(past_len, HALF), per_b(past_len, HALF)],
        out_specs=per_b(nq, HALF),
        out_shape=jax.ShapeDtypeStruct((bg, nq, HALF), BF16),
        compiler_params=_params(("parallel",)),
    )(iq, iw, aq, ik2, ak, av, cache_ik, cache_k, cache_v)


def _pad_w_in(w_in):
    d = w_in.shape[0]
    main = 8 * HALF
    ik = w_in[:, main:main + ATT_DH]
    iw = w_in[:, main + ATT_DH:]
    pad = jnp.zeros((d, LANES - iw.shape[1]), w_in.dtype)
    return jnp.concatenate([w_in[:, :main], ik, ik, iw, pad], axis=1).astype(BF16)


def _layer(x_p, x_s, c_p, c_s, past_k, past_v, past_ik, ret_s0, depth,
           w_cond, b_cond, f1g, f1u, f1d, ln1_g, ln1_b, w_in, w_out, ln2_g, ln2_b, f2g, f2u, f2d, ln3_g, ln3_b):
    alpha = (2.0 * depth) ** 0.25
    bp, tp, d = x_p.shape
    bs, ts, _ = x_s.shape
    past_len = past_k.shape[1]

    mod = _cond_call(jnp.concatenate([c_p, c_s], axis=0), w_cond, b_cond).reshape(bp + bs, 9, d)
    mod_p, mod_s = mod[:bp], mod[bp:]

    w_in_p = _pad_w_in(w_in)
    w_out_b = w_out.astype(BF16)
    f1 = (f1g.astype(BF16), f1u.astype(BF16), f1d.astype(BF16), ln1_g, ln1_b)
    f2 = (f2g.astype(BF16), f2u.astype(BF16), f2d.astype(BF16), ln3_g, ln3_b)

    tt_p = min(512, tp)
    bb_s = min(bs, max(1, 512 // ts))
    geo_p = dict(bb=1, tt=tt_p)
    geo_s = dict(bb=bb_s, tt=ts)

    x1_p = _block_call(x_p, mod_p, *f1, alpha=alpha, mod_idx=(0, 1, 2), **geo_p)
    x1_s = _block_call(x_s, mod_s, *f1, alpha=alpha, mod_idx=(0, 1, 2), **geo_s)

    tabs_p = _rope_tables(jnp.arange(tp, dtype=I32))
    tabs_s = _rope_tables(past_len + jnp.arange(ts, dtype=I32))
    (rq, rk, rv, rg, aq, ak_p, av_p, iq, ik2_p, iwt, avt) = _mixer_in_call(
        x1_p, mod_p, w_in_p, tabs_p, bb=1, tt=KEY_CHUNK, prompt_layout=True)
    ret_chunk = min(256, tp)
    oret_p, state_p = _retention_call(rq, rk, rv, rg, jnp.zeros((bp, RET_HEADS, RET_DK, RET_DK), F32),
                                      chunk=ret_chunk)
    oatt_p = _dsa_prompt_call(iq, iwt, aq, ik2_p, ak_p, avt)

    (rq, rk, rv, rg, aq, ak_s, av_s, iq, ik2_s, iw) = _mixer_in_call(
        x1_s, mod_s, w_in_p, tabs_s, prompt_layout=False, **geo_s)
    oret_s, state_s = _retention_call(rq, rk, rv, rg, ret_s0, chunk=ts)
    oatt_s = _dsa_sample_call(iq, iw, aq, ik2_s, ak_s, av_s, past_ik,
                              past_k.reshape(bs, past_len, HALF), past_v.reshape(bs, past_len, HALF))

    y_p = _block_call(x1_p, mod_p, *f2, alpha=alpha, mod_idx=(6, 7, 8), mix=(oret_p, oatt_p, w_out_b, ln2_g, ln2_b),
                      **geo_p)
    y_s = _block_call(x1_s, mod_s, *f2, alpha=alpha, mod_idx=(6, 7, 8), mix=(oret_s, oatt_s, w_out_b, ln2_g, ln2_b),
                      **geo_s)

    heads = lambda a: a.reshape(a.shape[0], a.shape[1], ATT_HEADS, ATT_DH)
    st_p = (heads(ak_p), heads(av_p), ik2_p[..., :ATT_DH], state_p)
    st_s = (heads(ak_s), heads(av_s), ik2_s[..., :ATT_DH], state_s)
    return y_p, y_s, st_p, st_s


def kernel(x_prompt, x_sample, c_prompt, c_sample, cache_k, cache_v, cache_idx_k, state_ret, w_cond, b_cond, ffn1_w_gate, ffn1_w_up, ffn1_w_down, ln1_g, ln1_b, w_in, w_out, ln2_g, ln2_b, ffn2_w_gate, ffn2_w_up, ffn2_w_down, ln3_g, ln3_b):
    depth = w_cond.shape[0]
    y_p, y_s = x_prompt, x_sample
    st_p, st_s = [], []
    for l in range(depth):
        y_p, y_s, sp, ss = _layer(
            y_p, y_s, c_prompt, c_sample, cache_k[l], cache_v[l], cache_idx_k[l], state_ret[l], depth,
            w_cond[l], b_cond[l], ffn1_w_gate[l], ffn1_w_up[l], ffn1_w_down[l], ln1_g[l], ln1_b[l],
            w_in[l], w_out[l], ln2_g[l], ln2_b[l], ffn2_w_gate[l], ffn2_w_up[l], ffn2_w_down[l], ln3_g[l], ln3_b[l])
        st_p.append(sp)
        st_s.append(ss)
    stack = lambda sts, i: jnp.stack([s[i] for s in sts])
    return (y_p, y_s, stack(st_p, 0), stack(st_p, 1), stack(st_p, 2), stack(st_p, 3),
            stack(st_s, 0), stack(st_s, 1), stack(st_s, 2), stack(st_s, 3))
```

```python
import functools
import math

import jax
import jax.numpy as jnp
import numpy as np
from jax import lax
from jax.experimental import pallas as pl
from jax.experimental.pallas import tpu as pltpu

F32 = jnp.float32
BF16 = jnp.bfloat16
I32 = jnp.int32

RET_HEADS = 4
RET_DK = 128
RET_ROPE_BASE = 10000.0
ATT_HEADS = 8
ATT_DH = 64
ATT_ROT = 16
IDX_HEADS = 8
ROPE_THETA = 500000.0
MASK_CHUNK = 64
TOPK_MAX = 256
LN_EPS = 1e-5
NEG = -1e30
INT_MIN = -2 ** 31
HALF = 512
LANES = 128
N_TABS = 13
KEY_CHUNK = 256
ATT_SUB_KEYS = 128
ATT_LOOKAHEAD = 8
ONES_ROWS = 16
VMEM_LIMIT = 56 * 1024 * 1024

_NT = (((1,), (1,)), ((), ()))
_TN = (((0,), (0,)), ((), ()))


def _params(sem):
    return pltpu.CompilerParams(dimension_semantics=sem, vmem_limit_bytes=VMEM_LIMIT)


def _layer_norm(v, g, b):
    mu = jnp.mean(v, -1, keepdims=True)
    d = v - mu
    var = jnp.mean(d * d, -1, keepdims=True)
    return d * lax.rsqrt(var + LN_EPS) * g + b


def _silu(a):
    return a * jax.nn.sigmoid(a)


def _resident(shape):
    nd = len(shape)
    return pl.BlockSpec(shape, lambda *_: (0,) * nd, pipeline_mode=pl.Buffered(1))


def _cond_kernel(c_ref, w_ref, b_ref, o_ref):
    s = _silu(c_ref[...])
    o_ref[...] = jnp.dot(s, w_ref[...], preferred_element_type=F32,
                         precision=lax.Precision.HIGHEST) + b_ref[...]


def _cond_call(c, w, b):
    n_rows, d = c.shape
    n = w.shape[1]
    tn = 1152
    return pl.pallas_call(
        _cond_kernel,
        grid=(n // tn,),
        in_specs=[pl.BlockSpec((n_rows, d), lambda i: (0, 0)),
                  pl.BlockSpec((d, tn), lambda i: (0, i)),
                  pl.BlockSpec((1, tn), lambda i: (0, i))],
        out_specs=pl.BlockSpec((n_rows, tn), lambda i: (0, i)),
        out_shape=jax.ShapeDtypeStruct((n_rows, n), F32),
        compiler_params=_params(("parallel",)),
        name="cond",
    )(c, w, b.reshape(1, n))


def _block_kernel(*refs, alpha, with_mix, mod_idx, n_chunks):
    if with_mix:
        (x_ref, mod_ref, oret_ref, oatt_ref, wout_ref, g2_ref, b2_ref,
         wg_ref, wu_ref, wd_ref, g_ref, b_ref, o_ref) = refs
    else:
        x_ref, mod_ref, wg_ref, wu_ref, wd_ref, g_ref, b_ref, o_ref = refs
    x = x_ref[...]
    bb, tt, d = x.shape
    rows = bb * tt
    mod = mod_ref[...]
    if with_mix:
        mixed = (jnp.dot(oret_ref[...].reshape(rows, HALF), wout_ref[:HALF, :], preferred_element_type=F32)
                 + jnp.dot(oatt_ref[...].reshape(rows, HALF), wout_ref[HALF:, :], preferred_element_type=F32))
        x = _layer_norm(alpha * x + (1.0 + mod[:, 5:6, :]) * mixed.reshape(bb, tt, d), g2_ref[...], b2_ref[...])
    i_sh, i_sc, i_gt = mod_idx
    h = (x * (1.0 + mod[:, i_sc:i_sc + 1, :]) + mod[:, i_sh:i_sh + 1, :]).reshape(rows, d).astype(BF16)
    dff = wg_ref.shape[1]
    ck = dff // n_chunks
    f = None
    for i in range(n_chunks):
        a = jnp.dot(h, wg_ref[:, i * ck:(i + 1) * ck], preferred_element_type=F32)
        u = jnp.dot(h, wu_ref[:, i * ck:(i + 1) * ck], preferred_element_type=F32)
        act = (_silu(a) * u).astype(BF16)
        part = jnp.dot(act, wd_ref[i * ck:(i + 1) * ck, :], preferred_element_type=F32)
        f = part if f is None else f + part
    y = alpha * x + (0.5 * (1.0 + mod[:, i_gt:i_gt + 1, :])) * f.reshape(bb, tt, d)
    o_ref[...] = _layer_norm(y, g_ref[...], b_ref[...])


def _block_call(x, mod, wg, wu, wd, g, b, *, alpha, mod_idx, bb, tt, mix=None):
    bg, tg, d = x.shape
    dff = wg.shape[1]
    row_spec = lambda w: pl.BlockSpec((bb, tt, w), lambda i, j: (i, j, 0))
    mod_spec = pl.BlockSpec((bb, 9, d), lambda i, j: (i, 0, 0))
    args = [x, mod]
    specs = [row_spec(d), mod_spec]
    if mix is not None:
        oret, oatt, wout, g2, b2 = mix
        args += [oret, oatt, wout, g2.reshape(1, d), b2.reshape(1, d)]
        specs += [row_spec(HALF), row_spec(HALF), _resident((d, d)), _resident((1, d)), _resident((1, d))]
    args += [wg, wu, wd, g.reshape(1, d), b.reshape(1, d)]
    specs += [_resident((d, dff)), _resident((d, dff)), _resident((dff, d)), _resident((1, d)), _resident((1, d))]
    return pl.pallas_call(
        functools.partial(_block_kernel, alpha=alpha, with_mix=mix is not None, mod_idx=mod_idx, n_chunks=2),
        grid=(bg // bb, tg // tt),
        in_specs=specs,
        out_specs=row_spec(d),
        out_shape=jax.ShapeDtypeStruct((bg, tg, d), F32),
        compiler_params=_params(("parallel", "parallel")),
        name="ffn_block" if mix is None else "mix_ffn_block",
    )(*args)


def _rope_tables(pos):
    posf = pos.astype(F32)[:, None]
    inv_r = 1.0 / (RET_ROPE_BASE ** (jnp.arange(0, RET_DK, 2, dtype=F32) / RET_DK))
    ang = posf * inv_r[None, :]
    cos_r, sin_r = jnp.cos(ang), jnp.sin(ang)
    cr = jnp.concatenate([cos_r, cos_r], -1)
    sr = jnp.concatenate([-sin_r, sin_r], -1)
    kscale = RET_DK ** -0.5
    inv_a = 1.0 / (ROPE_THETA ** (jnp.arange(0, ATT_ROT, 2, dtype=F32) / ATT_ROT))
    ang = posf * inv_a[None, :]
    cos_a, sin_a = jnp.cos(ang), jnp.sin(ang)
    t = pos.shape[0]
    half = ATT_ROT // 2
    rest = ATT_DH - ATT_ROT
    c64 = jnp.concatenate([cos_a, cos_a, jnp.ones((t, rest), F32)], -1)
    s1 = jnp.concatenate([-sin_a, jnp.zeros((t, ATT_DH - half), F32)], -1)
    s2 = jnp.concatenate([jnp.zeros((t, half), F32), sin_a, jnp.zeros((t, rest), F32)], -1)
    ca, s1a, s2a = (jnp.concatenate([v, v], -1) for v in (c64, s1, s2))
    iscale = ATT_DH ** -0.5
    qscale = ATT_DH ** -0.5 * math.log2(math.e)
    return jnp.stack([cr, sr, cr * kscale, sr * kscale,
                      ca * iscale, s1a * iscale, s2a * iscale, ca, s1a, s2a,
                      ca * qscale, s1a * qscale, s2a * qscale])


def _mixer_in_kernel(*refs, prompt_layout):
    x_ref, mod_ref, w_ref, tab_ref = refs[:4]
    outs = refs[4:]
    rq_ref, rk_ref, rv_ref, rg_ref, aq_ref, ak_ref, av_ref, iq_ref, ik2_ref, iw_ref = outs[:10]
    x = x_ref[...]
    bb, tt, d = x.shape
    rows = bb * tt
    mod = mod_ref[...]
    h = (x * (1.0 + mod[:, 4:5, :]) + mod[:, 3:4, :]).reshape(rows, d).astype(BF16)

    def seg(i, width=HALF):
        return jnp.dot(h, w_ref[:, HALF * i:HALF * i + width], preferred_element_type=F32)

    def tab(i):
        return tab_ref[i][None]

    def rope_ret_col(zc, ci, si):
        r = pltpu.roll(zc, 64, 1)
        return zc.reshape(bb, tt, LANES) * tab(ci) + r.reshape(bb, tt, LANES) * tab(si)

    def rope_att_col(zc, ci):
        r1 = pltpu.roll(zc, LANES - ATT_ROT // 2, 1)
        r2 = pltpu.roll(zc, ATT_ROT // 2, 1)
        return (zc.reshape(bb, tt, LANES) * tab(ci) + r1.reshape(bb, tt, LANES) * tab(ci + 1)
                + r2.reshape(bb, tt, LANES) * tab(ci + 2))

    def cols(z, fn):
        return jnp.concatenate([fn(z[:, LANES * g:LANES * (g + 1)]) for g in range(z.shape[1] // LANES)], axis=-1)

    rq_ref[...] = cols(seg(0), lambda zc: rope_ret_col(zc, 0, 1)).astype(BF16)
    rk_ref[...] = cols(seg(1), lambda zc: rope_ret_col(zc, 2, 3)).astype(BF16)
    rv_ref[...] = seg(2).reshape(bb, tt, HALF).astype(BF16)
    rg_ref[...] = seg(3).reshape(bb, tt, HALF).astype(BF16)
    aq_ref[...] = cols(seg(4), lambda zc: rope_att_col(zc, 10)).astype(BF16)
    ak_ref[...] = cols(seg(5), lambda zc: rope_att_col(zc, 7))
    av = seg(6)
    av_ref[...] = av.reshape(bb, tt, HALF)
    iq_ref[...] = cols(seg(7), lambda zc: rope_att_col(zc, 4)).astype(BF16)
    tail = seg(8, 2 * LANES)
    ik2_ref[...] = rope_att_col(tail[:, :LANES], 7)
    iw = tail[:, LANES:] * (IDX_HEADS ** -0.5)
    if prompt_layout:
        avt_ref = outs[10]
        iw_ref[0] = iw.T[:IDX_HEADS, :]
        avt = av.T.astype(BF16)
        for s in range(tt // ATT_SUB_KEYS):
            avt_ref[0, s] = avt[:, s * ATT_SUB_KEYS:(s + 1) * ATT_SUB_KEYS]
    else:
        iw_ref[...] = iw.reshape(bb, tt, LANES)


def _mixer_in_call(x, mod, w_in_p, tabs, *, bb, tt, prompt_layout):
    bg, tg, d = x.shape
    row = lambda w: pl.BlockSpec((bb, tt, w), lambda i, j: (i, j, 0))
    sds = lambda w, dt: jax.ShapeDtypeStruct((bg, tg, w), dt)
    out_specs = [row(HALF)] * 8 + [row(LANES)]
    out_shape = [sds(HALF, BF16), sds(HALF, BF16), sds(HALF, BF16), sds(HALF, BF16), sds(HALF, BF16),
                 sds(HALF, F32), sds(HALF, F32), sds(HALF, BF16), sds(LANES, F32)]
    if prompt_layout:
        assert bb == 1 and tt == KEY_CHUNK
        out_specs += [pl.BlockSpec((1, IDX_HEADS, tt), lambda i, j: (i, 0, j)),
                      pl.BlockSpec((1, tt // ATT_SUB_KEYS, HALF, ATT_SUB_KEYS), lambda i, j: (i, j, 0, 0))]
        out_shape += [jax.ShapeDtypeStruct((bg, IDX_HEADS, tg), F32),
                      jax.ShapeDtypeStruct((bg, tg // ATT_SUB_KEYS, HALF, ATT_SUB_KEYS), BF16)]
    else:
        out_specs += [row(LANES)]
        out_shape += [sds(LANES, F32)]
    return pl.pallas_call(
        functools.partial(_mixer_in_kernel, prompt_layout=prompt_layout),
        grid=(bg // bb, tg // tt),
        in_specs=[row(d), pl.BlockSpec((bb, 9, d), lambda i, j: (i, 0, 0)),
                  _resident(w_in_p.shape),
                  pl.BlockSpec((N_TABS, tt, LANES), lambda i, j: (0, j, 0))],
        out_specs=out_specs,
        out_shape=out_shape,
        compiler_params=_params(("parallel", "parallel")),
        name="mixer_in",
    )(x, mod, w_in_p, tabs)


def _retention_tables(c):
    lg = jnp.log(1.0 - 2.0 ** (-5.0 - jnp.arange(RET_HEADS, dtype=F32)))
    n = jnp.arange(c, dtype=F32)
    rel = n[:, None] - n[None, :]
    dmat = jnp.where(rel[None] >= 0, jnp.exp(lg[:, None, None] * jnp.maximum(rel, 0.0)[None]), 0.0)
    cross = jnp.exp(lg[:, None] * (n[None, :] + 1.0))
    kdec = jnp.exp(lg[:, None] * (c - 1.0 - n)[None, :])
    sdec = jnp.exp(lg * c)
    bc = lambda v: jnp.broadcast_to(v[:, :, None], (RET_HEADS, c, LANES))
    return dmat, bc(cross), bc(kdec), jnp.broadcast_to(sdec[:, None, None], (RET_HEADS, 8, LANES))


def _retention_kernel(rq_ref, rk_ref, rv_ref, rg_ref, s0_ref, dmat_ref, cdec_ref, kdec_ref, sdec_ref,
                      o_ref, sout_ref, s_scr):
    @pl.when(pl.program_id(1) == 0)
    def _():
        s_scr[...] = s0_ref[0]

    for hd in range(RET_HEADS):
        sl = slice(RET_DK * hd, RET_DK * (hd + 1))
        q = rq_ref[0, :, sl]
        k = rk_ref[0, :, sl]
        v = rv_ref[0, :, sl]
        state = s_scr[hd]
        scores = lax.dot_general(q, k, _NT, preferred_element_type=F32) * dmat_ref[hd]
        inner = jnp.dot(scores.astype(BF16), v, preferred_element_type=F32)
        cross = jnp.dot(q, state.astype(BF16), preferred_element_type=F32) * cdec_ref[hd]
        kd = (k.astype(F32) * kdec_ref[hd]).astype(BF16)
        s_new = sdec_ref[hd, 0:1, :] * state + lax.dot_general(kd, v, _TN, preferred_element_type=F32)
        s_scr[hd] = s_new
        o = inner + cross
        mu = jnp.mean(o, -1, keepdims=True)
        dlt = o - mu
        var = jnp.mean(dlt * dlt, -1, keepdims=True)
        o_ref[0, :, sl] = (dlt * lax.rsqrt(var + LN_EPS) * _silu(rg_ref[0, :, sl].astype(F32))).astype(BF16)
    sout_ref[0] = s_scr[...]


def _retention_call(rq, rk, rv, rg, s0, *, chunk):
    bg, tg, _ = rq.shape
    dmat, cdec, kdec, sdec = _retention_tables(chunk)
    row = pl.BlockSpec((1, chunk, HALF), lambda b, c: (b, c, 0))
    st = pl.BlockSpec((1, RET_HEADS, RET_DK, RET_DK), lambda b, c: (b, 0, 0, 0))
    return pl.pallas_call(
        _retention_kernel,
        grid=(bg, tg // chunk),
        in_specs=[row, row, row, row, st, _resident(dmat.shape), _resident(cdec.shape), _resident(kdec.shape),
                  _resident(sdec.shape)],
        out_specs=[row, st],
        out_shape=[jax.ShapeDtypeStruct((bg, tg, HALF), BF16),
                   jax.ShapeDtypeStruct((bg, RET_HEADS, RET_DK, RET_DK), F32)],
        scratch_shapes=[pltpu.VMEM((RET_HEADS, RET_DK, RET_DK), F32)],
        compiler_params=_params(("parallel", "arbitrary")),
        name="retention",
    )(rq, rk, rv, rg, s0, dmat, cdec, kdec, sdec)


def _sortable_key(score):
    bits = pltpu.bitcast(score, I32)
    return bits ^ ((bits >> 31) & 0x7FFFFFFF)


def _kth_largest_key(count_ge, topk, like):
    lo = jnp.where(count_ge(jnp.zeros_like(like)) >= topk, 0, INT_MIN).astype(I32)

    def body(i, lo):
        cand = lo + jnp.left_shift(jnp.int32(1), 30 - i)
        return jnp.where(count_ge(cand) >= topk, cand, lo)

    return lax.fori_loop(0, 31, body, lo)


def _tie_cut(count_tie_below, need, n_pos_bits, like):
    def body(i, j0):
        cand = j0 + jnp.left_shift(jnp.int32(1), n_pos_bits - 1 - i)
        return jnp.where(count_tie_below(cand) < need, cand, j0)

    return lax.fori_loop(0, n_pos_bits, body, jnp.zeros_like(like)) + 1


def _dsa_prompt_kernel(iq_ref, iwt_ref, aq_ref, ik2_ref, ak_ref, avt_ref, o_ref,
                       keys_scr, iqm_scr, aqm_scr, m_scr, acc_scr, outt_scr, *, topk, seq_len):
    qb = kc = KEY_CHUNK
    j = pl.program_id(1)
    nch = j + 1
    lane_q = lax.broadcasted_iota(I32, (1, qb), 1)
    limit = ((j * qb + lane_q) // MASK_CHUNK + 1) * MASK_CHUNK
    row_pos = lax.broadcasted_iota(I32, (kc, qb), 0)

    lane_head = lax.broadcasted_iota(I32, (qb, LANES), 1) // ATT_DH
    for hd in range(ATT_HEADS):
        col = slice(LANES * (hd // 2), LANES * (hd // 2 + 1))
        keep = lane_head == (hd % 2)
        iqm_scr[hd] = jnp.where(keep, iq_ref[0, :, col], jnp.zeros((), BF16))
        aqm_scr[hd] = jnp.where(keep, aq_ref[0, :, col], jnp.zeros((), BF16))

    iwt = iwt_ref[0]

    def index_chunk(c, carry):
        off = pl.multiple_of(c * kc, kc)
        ikc = ik2_ref[0, pl.ds(off, kc), :].astype(BF16)
        acc = jnp.zeros((kc, qb), F32)
        for hd in range(IDX_HEADS):
            s = lax.dot_general(ikc, iqm_scr[hd], _NT, preferred_element_type=F32)
            acc = acc + iwt[hd:hd + 1, :] * jnp.maximum(s, 0.0)
        keys_scr[c] = jnp.where(row_pos + off < limit, _sortable_key(acc), INT_MIN)
        return carry

    lax.fori_loop(0, nch, index_chunk, 0)

    def count(pred):
        def body(c, acc):
            m = pred(keys_scr[c], row_pos + c * kc)
            return acc + jnp.sum(m.reshape(kc // 8, 8, qb), axis=0)
        acc = lax.fori_loop(0, nch, body, jnp.zeros((8, qb), I32))
        return jnp.sum(acc, axis=0, keepdims=True)

    one, zero = jnp.int32(1), jnp.int32(0)
    thr = _kth_largest_key(lambda t: count(lambda k, p: jnp.where(k >= t, one, zero)), topk, lane_q)
    n_gt = count(lambda k, p: jnp.where(k > thr, one, zero))
    n_eq = count(lambda k, p: jnp.where(k == thr, one, zero))
    need = topk - n_gt
    excess = jnp.where(thr == INT_MIN, 0, n_eq - need)
    take_all = jnp.where(thr == INT_MIN, 0, seq_len + 1).astype(I32)

    def tie_break():
        below = lambda jc: count(lambda k, p: jnp.where(k == thr, jnp.where(p < jc, one, zero), zero))
        cut = _tie_cut(below, need, max(1, int(math.ceil(math.log2(seq_len)))), lane_q)
        return jnp.where(excess > 0, cut, take_all)

    cut = lax.cond(jnp.max(excess) > 0, tie_break, lambda: take_all)

    m_scr[...] = jnp.full(m_scr.shape, NEG, F32)
    acc_scr[...] = jnp.zeros(acc_scr.shape, F32)
    sub = ATT_SUB_KEYS
    ones_rows = jnp.ones((ONES_ROWS, sub), BF16)
    sub_pos = lax.broadcasted_iota(I32, (sub, qb), 0)

    def attend(c, carry):
        off = pl.multiple_of(c * kc, kc)
        steps = [(s, hd) for s in range(kc // sub) for hd in range(ATT_HEADS)]
        biases = {}

        def logits(s, hd):
            if s not in biases:
                k = keys_scr[c, s * sub:(s + 1) * sub, :]
                pos = sub_pos + (c * kc + s * sub)
                biases[s] = jnp.where(k > thr, 0.0, jnp.where(k == thr, jnp.where(pos < cut, 0.0, NEG), NEG))
            col = slice(LANES * (hd // 2), LANES * (hd // 2 + 1))
            kcol = ak_ref[0, pl.ds(off + s * sub, sub), col].astype(BF16)
            return lax.dot_general(kcol, aqm_scr[hd], _NT, preferred_element_type=F32) + biases[s]

        ahead = [logits(*steps[i]) for i in range(ATT_LOOKAHEAD)]
        for i, (s, hd) in enumerate(steps):
            lg = ahead.pop(0)
            if i + ATT_LOOKAHEAD < len(steps):
                ahead.append(logits(*steps[i + ATT_LOOKAHEAD]))
            m_old = m_scr[hd]
            m_new = jnp.maximum(m_old, jnp.max(lg, axis=0, keepdims=True))
            m_scr[hd] = m_new
            p = jnp.exp2(lg - m_new).astype(BF16)
            vt = jnp.concatenate([avt_ref[0, c * (kc // sub) + s, ATT_DH * hd:ATT_DH * (hd + 1), :], ones_rows],
                                 axis=0)
            acc_scr[hd] = jnp.exp2(m_old - m_new) * acc_scr[hd] + jnp.dot(vt, p, preferred_element_type=F32)
        return carry

    lax.fori_loop(0, nch, attend, 0)
    for hd in range(ATT_HEADS):
        acc = acc_scr[hd]
        outt_scr[ATT_DH * hd:ATT_DH * (hd + 1), :] = acc[:ATT_DH] / acc[ATT_DH:ATT_DH + 1]
    o_ref[0] = outt_scr[...].T.astype(BF16)


def _dsa_prompt_call(iq, iwt, aq, ik2, ak, avt):
    bg, tg, _ = iq.shape
    qb = KEY_CHUNK
    nc = tg // qb
    topk = min(TOPK_MAX, tg // 4)
    assert topk <= qb
    blk = pl.BlockSpec((1, qb, HALF), lambda b, j: (b, j, 0))
    return pl.pallas_call(
        functools.partial(_dsa_prompt_kernel, topk=topk, seq_len=tg),
        grid=(bg, nc),
        in_specs=[blk,
                  pl.BlockSpec((1, IDX_HEADS, qb), lambda b, j: (b, 0, j)),
                  blk,
                  pl.BlockSpec((1, tg, LANES), lambda b, j: (b, 0, 0)),
                  pl.BlockSpec((1, tg, HALF), lambda b, j: (b, 0, 0)),
                  pl.BlockSpec((1, tg // ATT_SUB_KEYS, HALF, ATT_SUB_KEYS), lambda b, j: (b, 0, 0, 0))],
        out_specs=blk,
        out_shape=jax.ShapeDtypeStruct((bg, tg, HALF), BF16),
        scratch_shapes=[pltpu.VMEM((nc, qb, qb), I32),
                        pltpu.VMEM((ATT_HEADS, qb, LANES), BF16), pltpu.VMEM((ATT_HEADS, qb, LANES), BF16),
                        pltpu.VMEM((ATT_HEADS, 1, qb), F32), pltpu.VMEM((ATT_HEADS, ATT_DH + ONES_ROWS, qb), F32),
                        pltpu.VMEM((HALF, qb), F32)],
        compiler_params=_params(("parallel", "arbitrary")),
        name="dsa_prompt",
    )(iq, iwt, aq, ik2, ak, avt)


def _dsa_sample_kernel(iq_ref, iw_ref, aq_ref, ik2n_ref, akn_ref, avn_ref, cik_ref, ck_ref, cv_ref, o_ref,
                       *, topk, past_len):
    nq = iq_ref.shape[1]
    n_new = nq
    total = past_len + n_new
    iq = iq_ref[0]
    aq = aq_ref[0]
    iw = iw_ref[0]

    iq_rows = jnp.concatenate([iq[:, ATT_DH * hd:ATT_DH * (hd + 1)] for hd in range(IDX_HEADS)], axis=0)
    ik_c = cik_ref[0].astype(BF16)
    ik_n = ik2n_ref[0][:, :ATT_DH].astype(BF16)

    def index_scores(ik):
        s = lax.dot_general(iq_rows, ik, _NT, preferred_element_type=F32)
        acc = jnp.zeros((nq, ik.shape[0]), F32)
        for hd in range(IDX_HEADS):
            acc = acc + iw[:, hd:hd + 1] * jnp.maximum(s[nq * hd:nq * (hd + 1), :], 0.0)
        return acc

    qpos = past_len + lax.broadcasted_iota(I32, (nq, 1), 0)
    limit = (qpos // MASK_CHUNK + 1) * MASK_CHUNK
    pos_c = lax.broadcasted_iota(I32, (nq, past_len), 1)
    pos_n = past_len + lax.broadcasted_iota(I32, (nq, n_new), 1)
    key_c = jnp.where(pos_c < limit, _sortable_key(index_scores(ik_c)), INT_MIN)
    key_n = jnp.where(pos_n < limit, _sortable_key(index_scores(ik_n)), INT_MIN)

    def count(pred):
        return (jnp.sum(pred(key_c, pos_c), axis=1, keepdims=True)
                + jnp.sum(pred(key_n, pos_n), axis=1, keepdims=True))

    one, zero = jnp.int32(1), jnp.int32(0)
    thr = _kth_largest_key(lambda t: count(lambda k, p: jnp.where(k >= t, one, zero)), topk, qpos)
    n_gt = count(lambda k, p: jnp.where(k > thr, one, zero))
    n_eq = count(lambda k, p: jnp.where(k == thr, one, zero))
    need = topk - n_gt
    excess = jnp.where(thr == INT_MIN, 0, n_eq - need)
    take_all = jnp.where(thr == INT_MIN, 0, total + 1).astype(I32)

    def tie_break():
        below = lambda jc: count(lambda k, p: jnp.where(k == thr, jnp.where(p < jc, one, zero), zero))
        cut = _tie_cut(below, need, max(1, int(math.ceil(math.log2(total)))), qpos)
        return jnp.where(excess > 0, cut, take_all)

    cut = lax.cond(jnp.max(excess) > 0, tie_break, lambda: take_all)

    def bias(k, pos):
        return jnp.where(k > thr, 0.0, jnp.where(k == thr, jnp.where(pos < cut, 0.0, NEG), NEG))

    bias_c, bias_n = bias(key_c, pos_c), bias(key_n, pos_n)

    lane_head = lax.broadcasted_iota(I32, (nq, HALF), 1) // ATT_DH
    q_bd = jnp.concatenate([jnp.where(lane_head == hd, aq, jnp.zeros((), BF16)) for hd in range(ATT_HEADS)], axis=0)
    rows = ATT_HEADS * nq

    def logits(k, b):
        lg = lax.dot_general(q_bd, k.astype(BF16), _NT, preferred_element_type=F32)
        return (lg.reshape(ATT_HEADS, nq, k.shape[0]) + b[None]).reshape(rows, k.shape[0])

    lg_c = logits(ck_ref[0], bias_c)
    lg_n = logits(akn_ref[0], bias_n)
    m = jnp.maximum(jnp.max(lg_c, axis=1, keepdims=True), jnp.max(lg_n, axis=1, keepdims=True))
    p_c = jnp.exp2(lg_c - m)
    p_n = jnp.exp2(lg_n - m)
    l = jnp.sum(p_c, axis=1, keepdims=True) + jnp.sum(p_n, axis=1, keepdims=True)
    res = (jnp.dot(p_c.astype(BF16), cv_ref[0].astype(BF16), preferred_element_type=F32)
           + jnp.dot(p_n.astype(BF16), avn_ref[0].astype(BF16), preferred_element_type=F32)) / l
    res = res.reshape(ATT_HEADS, nq, HALF)
    out = jnp.zeros((nq, HALF), F32)
    for hd in range(ATT_HEADS):
        out = out + jnp.where(lane_head == hd, res[hd], 0.0)
    o_ref[0] = out.astype(BF16)


def _dsa_sample_call(iq, iw, aq, ik2, ak, av, cache_ik, cache_k, cache_v):
    bg, nq, _ = iq.shape
    past_len = cache_k.shape[1]
    topk = min(TOPK_MAX, (past_len + nq) // 4)
    per_b = lambda n, w: pl.BlockSpec((1, n, w), lambda b: (b, 0, 0))
    return pl.pallas_call(
        functools.partial(_dsa_sample_kernel, topk=topk, past_len=past_len),
        grid=(bg,),
        in_specs=[per_b(nq, HALF), per_b(nq, LANES), per_b(nq, HALF), per_b(nq, LANES), per_b(nq, HALF),
                  per_b(nq, HALF), per_b(past_len, ATT_DH), per_b(past_len, HALF), per_b(past_len, HALF)],
        out_specs=per_b(nq, HALF),
        out_shape=jax.ShapeDtypeStruct((bg, nq, HALF), BF16),
        compiler_params=_params(("parallel",)),
        name="dsa_sample",
    )(iq, iw, aq, ik2, ak, av, cache_ik, cache_k, cache_v)


def _pad_w_in(w_in):
    d = w_in.shape[0]
    main = 8 * HALF
    ik = w_in[:, main:main + ATT_DH]
    iw = w_in[:, main + ATT_DH:]
    pad = jnp.zeros((d, LANES - iw.shape[1]), w_in.dtype)
    return jnp.concatenate([w_in[:, :main], ik, ik, iw, pad], axis=1).astype(BF16)


def _layer(x_p, x_s, c_p, c_s, past_k, past_v, past_ik, ret_s0, depth,
           w_cond, b_cond, f1g, f1u, f1d, ln1_g, ln1_b, w_in, w_out, ln2_g, ln2_b, f2g, f2u, f2d, ln3_g, ln3_b):
    alpha = (2.0 * depth) ** 0.25
    bp, tp, d = x_p.shape
    bs, ts, _ = x_s.shape
    past_len = past_k.shape[1]

    mod = _cond_call(jnp.concatenate([c_p, c_s], axis=0), w_cond, b_cond).reshape(bp + bs, 9, d)
    mod_p, mod_s = mod[:bp], mod[bp:]

    w_in_p = _pad_w_in(w_in)
    w_out_b = w_out.astype(BF16)
    f1 = (f1g.astype(BF16), f1u.astype(BF16), f1d.astype(BF16), ln1_g, ln1_b)
    f2 = (f2g.astype(BF16), f2u.astype(BF16), f2d.astype(BF16), ln3_g, ln3_b)

    tt_p = min(512, tp)
    bb_s = min(bs, max(1, 512 // ts))
    geo_p = dict(bb=1, tt=tt_p)
    geo_s = dict(bb=bb_s, tt=ts)

    x1_p = _block_call(x_p, mod_p, *f1, alpha=alpha, mod_idx=(0, 1, 2), **geo_p)
    x1_s = _block_call(x_s, mod_s, *f1, alpha=alpha, mod_idx=(0, 1, 2), **geo_s)

    tabs_p = _rope_tables(jnp.arange(tp, dtype=I32))
    tabs_s = _rope_tables(past_len + jnp.arange(ts, dtype=I32))
    (rq, rk, rv, rg, aq, ak_p, av_p, iq, ik2_p, iwt, avt) = _mixer_in_call(
        x1_p, mod_p, w_in_p, tabs_p, bb=1, tt=KEY_CHUNK, prompt_layout=True)
    ret_chunk = min(256, tp)
    oret_p, state_p = _retention_call(rq, rk, rv, rg, jnp.zeros((bp, RET_HEADS, RET_DK, RET_DK), F32),
                                      chunk=ret_chunk)
    oatt_p = _dsa_prompt_call(iq, iwt, aq, ik2_p, ak_p, avt)

    (rq, rk, rv, rg, aq, ak_s, av_s, iq, ik2_s, iw) = _mixer_in_call(
        x1_s, mod_s, w_in_p, tabs_s, prompt_layout=False, **geo_s)
    oret_s, state_s = _retention_call(rq, rk, rv, rg, ret_s0, chunk=ts)
    oatt_s = _dsa_sample_call(iq, iw, aq, ik2_s, ak_s, av_s, past_ik,
                              past_k.reshape(bs, past_len, HALF), past_v.reshape(bs, past_len, HALF))

    y_p = _block_call(x1_p, mod_p, *f2, alpha=alpha, mod_idx=(6, 7, 8), mix=(oret_p, oatt_p, w_out_b, ln2_g, ln2_b),
                      **geo_p)
    y_s = _block_call(x1_s, mod_s, *f2, alpha=alpha, mod_idx=(6, 7, 8), mix=(oret_s, oatt_s, w_out_b, ln2_g, ln2_b),
                      **geo_s)

    heads = lambda a: a.reshape(a.shape[0], a.shape[1], ATT_HEADS, ATT_DH)
    st_p = (heads(ak_p), heads(av_p), ik2_p[..., :ATT_DH], state_p)
    st_s = (heads(ak_s), heads(av_s), ik2_s[..., :ATT_DH], state_s)
    return y_p, y_s, st_p, st_s


def kernel(x_prompt, x_sample, c_prompt, c_sample, cache_k, cache_v, cache_idx_k, state_ret, w_cond, b_cond, ffn1_w_gate, ffn1_w_up, ffn1_w_down, ln1_g, ln1_b, w_in, w_out, ln2_g, ln2_b, ffn2_w_gate, ffn2_w_up, ffn2_w_down, ln3_g, ln3_b):
    depth = w_cond.shape[0]
    y_p, y_s = x_prompt, x_sample
    st_p, st_s = [], []
    for l in range(depth):
        y_p, y_s, sp, ss = _layer(
            y_p, y_s, c_prompt, c_sample, cache_k[l], cache_v[l], cache_idx_k[l], state_ret[l], depth,
            w_cond[l], b_cond[l], ffn1_w_gate[l], ffn1_w_up[l], ffn1_w_down[l], ln1_g[l], ln1_b[l],
            w_in[l], w_out[l], ln2_g[l], ln2_b[l], ffn2_w_gate[l], ffn2_w_up[l], ffn2_w_down[l], ln3_g[l], ln3_b[l])
        st_p.append(sp)
        st_s.append(ss)
    stack = lambda sts, i: jnp.stack([s[i] for s in sts])
    return (y_p, y_s, stack(st_p, 0), stack(st_p, 1), stack(st_p, 2), stack(st_p, 3),
            stack(st_s, 0), stack(st_s, 1), stack(st_s, 2), stack(st_s, 3))
```

```python
import functools
import math

import jax
import jax.numpy as jnp
import numpy as np
from jax import lax
from jax.experimental import pallas as pl
from jax.experimental.pallas import tpu as pltpu

F32 = jnp.float32
BF16 = jnp.bfloat16
I32 = jnp.int32

RET_HEADS = 4
RET_DK = 128
RET_ROPE_BASE = 10000.0
ATT_HEADS = 8
ATT_DH = 64
ATT_ROT = 16
IDX_HEADS = 8
ROPE_THETA = 500000.0
MASK_CHUNK = 64
TOPK_MAX = 256
LN_EPS = 1e-5
NEG = -1e30
INT_MIN = -2 ** 31
HALF = 512
LANES = 128
N_TABS = 13
KEY_CHUNK = 256
ATT_SUB_KEYS = 128
ATT_LOOKAHEAD = 8
ONES_ROWS = 16
VMEM_LIMIT = 56 * 1024 * 1024

_NT = (((1,), (1,)), ((), ()))
_TN = (((0,), (0,)), ((), ()))


def _params(sem):
    return pltpu.CompilerParams(dimension_semantics=sem, vmem_limit_bytes=VMEM_LIMIT)


def _layer_norm(v, g, b):
    mu = jnp.mean(v, -1, keepdims=True)
    d = v - mu
    var = jnp.mean(d * d, -1, keepdims=True)
    return d * lax.rsqrt(var + LN_EPS) * g + b


def _silu(a):
    return a * jax.nn.sigmoid(a)


def _resident(shape):
    nd = len(shape)
    return pl.BlockSpec(shape, lambda *_: (0,) * nd, pipeline_mode=pl.Buffered(1))


def _cond_kernel(c_ref, w_ref, b_ref, o_ref):
    s = _silu(c_ref[...])
    o_ref[...] = jnp.dot(s, w_ref[...], preferred_element_type=F32,
                         precision=lax.Precision.HIGHEST) + b_ref[...]


def _cond_call(c, w, b):
    n_rows, d = c.shape
    n = w.shape[1]
    tn = 1152
    return pl.pallas_call(
        _cond_kernel,
        grid=(n // tn,),
        in_specs=[pl.BlockSpec((n_rows, d), lambda i: (0, 0)),
                  pl.BlockSpec((d, tn), lambda i: (0, i)),
                  pl.BlockSpec((1, tn), lambda i: (0, i))],
        out_specs=pl.BlockSpec((n_rows, tn), lambda i: (0, i)),
        out_shape=jax.ShapeDtypeStruct((n_rows, n), F32),
        compiler_params=_params(("parallel",)),
        name="cond",
    )(c, w, b.reshape(1, n))


def _block_kernel(*refs, alpha, with_mix, mod_idx, n_chunks):
    if with_mix:
        (x_ref, mod_ref, oret_ref, oatt_ref, wout_ref, g2_ref, b2_ref,
         wg_ref, wu_ref, wd_ref, g_ref, b_ref, o_ref) = refs
    else:
        x_ref, mod_ref, wg_ref, wu_ref, wd_ref, g_ref, b_ref, o_ref = refs
    x = x_ref[...]
    bb, tt, d = x.shape
    rows = bb * tt
    mod = mod_ref[...]
    if with_mix:
        mixed = (jnp.dot(oret_ref[...].reshape(rows, HALF), wout_ref[:HALF, :], preferred_element_type=F32)
                 + jnp.dot(oatt_ref[...].reshape(rows, HALF), wout_ref[HALF:, :], preferred_element_type=F32))
        x = _layer_norm(alpha * x + (1.0 + mod[:, 5:6, :]) * mixed.reshape(bb, tt, d), g2_ref[...], b2_ref[...])
    i_sh, i_sc, i_gt = mod_idx
    h = (x * (1.0 + mod[:, i_sc:i_sc + 1, :]) + mod[:, i_sh:i_sh + 1, :]).reshape(rows, d).astype(BF16)
    dff = wg_ref.shape[1]
    ck = dff // n_chunks
    f = None
    for i in range(n_chunks):
        a = jnp.dot(h, wg_ref[:, i * ck:(i + 1) * ck], preferred_element_type=F32)
        u = jnp.dot(h, wu_ref[:, i * ck:(i + 1) * ck], preferred_element_type=F32)
        act = (_silu(a) * u).astype(BF16)
        part = jnp.dot(act, wd_ref[i * ck:(i + 1) * ck, :], preferred_element_type=F32)
        f = part if f is None else f + part
    y = alpha * x + (0.5 * (1.0 + mod[:, i_gt:i_gt + 1, :])) * f.reshape(bb, tt, d)
    o_ref[...] = _layer_norm(y, g_ref[...], b_ref[...])


def _block_call(x, mod, wg, wu, wd, g, b, *, alpha, mod_idx, bb, tt, mix=None):
    bg, tg, d = x.shape
    dff = wg.shape[1]
    row_spec = lambda w: pl.BlockSpec((bb, tt, w), lambda i, j: (i, j, 0))
    mod_spec = pl.BlockSpec((bb, 9, d), lambda i, j: (i, 0, 0))
    args = [x, mod]
    specs = [row_spec(d), mod_spec]
    if mix is not None:
        oret, oatt, wout, g2, b2 = mix
        args += [oret, oatt, wout, g2.reshape(1, d), b2.reshape(1, d)]
        specs += [row_spec(HALF), row_spec(HALF), _resident((d, d)), _resident((1, d)), _resident((1, d))]
    args += [wg, wu, wd, g.reshape(1, d), b.reshape(1, d)]
    specs += [_resident((d, dff)), _resident((d, dff)), _resident((dff, d)), _resident((1, d)), _resident((1, d))]
    return pl.pallas_call(
        functools.partial(_block_kernel, alpha=alpha, with_mix=mix is not None, mod_idx=mod_idx, n_chunks=2),
        grid=(bg // bb, tg // tt),
        in_specs=specs,
        out_specs=row_spec(d),
        out_shape=jax.ShapeDtypeStruct((bg, tg, d), F32),
        compiler_params=_params(("parallel", "parallel")),
        name="ffn_block" if mix is None else "mix_ffn_block",
    )(*args)


def _rope_tables(pos):
    posf = pos.astype(F32)[:, None]
    inv_r = 1.0 / (RET_ROPE_BASE ** (jnp.arange(0, RET_DK, 2, dtype=F32) / RET_DK))
    ang = posf * inv_r[None, :]
    cos_r, sin_r = jnp.cos(ang), jnp.sin(ang)
    cr = jnp.concatenate([cos_r, cos_r], -1)
    sr = jnp.concatenate([-sin_r, sin_r], -1)
    kscale = RET_DK ** -0.5
    inv_a = 1.0 / (ROPE_THETA ** (jnp.arange(0, ATT_ROT, 2, dtype=F32) / ATT_ROT))
    ang = posf * inv_a[None, :]
    cos_a, sin_a = jnp.cos(ang), jnp.sin(ang)
    t = pos.shape[0]
    half = ATT_ROT // 2
    rest = ATT_DH - ATT_ROT
    c64 = jnp.concatenate([cos_a, cos_a, jnp.ones((t, rest), F32)], -1)
    s1 = jnp.concatenate([-sin_a, jnp.zeros((t, ATT_DH - half), F32)], -1)
    s2 = jnp.concatenate([jnp.zeros((t, half), F32), sin_a, jnp.zeros((t, rest), F32)], -1)
    ca, s1a, s2a = (jnp.concatenate([v, v], -1) for v in (c64, s1, s2))
    iscale = ATT_DH ** -0.5
    qscale = ATT_DH ** -0.5 * math.log2(math.e)
    return jnp.stack([cr, sr, cr * kscale, sr * kscale,
                      ca * iscale, s1a * iscale, s2a * iscale, ca, s1a, s2a,
                      ca * qscale, s1a * qscale, s2a * qscale])


def _mixer_in_kernel(*refs, prompt_layout):
    x_ref, mod_ref, w_ref, tab_ref = refs[:4]
    outs = refs[4:]
    rq_ref, rk_ref, rv_ref, rg_ref, aq_ref, ak_ref, av_ref, iq_ref, ik2_ref, iw_ref = outs[:10]
    x = x_ref[...]
    bb, tt, d = x.shape
    rows = bb * tt
    mod = mod_ref[...]
    h = (x * (1.0 + mod[:, 4:5, :]) + mod[:, 3:4, :]).reshape(rows, d).astype(BF16)

    def seg(i, width=HALF):
        return jnp.dot(h, w_ref[:, HALF * i:HALF * i + width], preferred_element_type=F32)

    def tab(i):
        return tab_ref[i][None]

    def rope_ret_col(zc, ci, si):
        r = pltpu.roll(zc, 64, 1)
        return zc.reshape(bb, tt, LANES) * tab(ci) + r.reshape(bb, tt, LANES) * tab(si)

    def rope_att_col(zc, ci):
        r1 = pltpu.roll(zc, LANES - ATT_ROT // 2, 1)
        r2 = pltpu.roll(zc, ATT_ROT // 2, 1)
        return (zc.reshape(bb, tt, LANES) * tab(ci) + r1.reshape(bb, tt, LANES) * tab(ci + 1)
                + r2.reshape(bb, tt, LANES) * tab(ci + 2))

    def cols(z, fn):
        return jnp.concatenate([fn(z[:, LANES * g:LANES * (g + 1)]) for g in range(z.shape[1] // LANES)], axis=-1)

    rq_ref[...] = cols(seg(0), lambda zc: rope_ret_col(zc, 0, 1)).astype(BF16)
    rk_ref[...] = cols(seg(1), lambda zc: rope_ret_col(zc, 2, 3)).astype(BF16)
    rv_ref[...] = seg(2).reshape(bb, tt, HALF).astype(BF16)
    rg_ref[...] = seg(3).reshape(bb, tt, HALF).astype(BF16)
    aq_ref[...] = cols(seg(4), lambda zc: rope_att_col(zc, 10)).astype(BF16)
    ak_ref[...] = cols(seg(5), lambda zc: rope_att_col(zc, 7))
    av = seg(6)
    av_ref[...] = av.reshape(bb, tt, HALF)
    iq_ref[...] = cols(seg(7), lambda zc: rope_att_col(zc, 4)).astype(BF16)
    tail = seg(8, 2 * LANES)
    ik2_ref[...] = rope_att_col(tail[:, :LANES], 7)
    iw = tail[:, LANES:] * (IDX_HEADS ** -0.5)
    if prompt_layout:
        avt_ref = outs[10]
        iw_ref[0] = iw.T[:IDX_HEADS, :]
        avt = av.T.astype(BF16)
        for s in range(tt // ATT_SUB_KEYS):
            avt_ref[0, s] = avt[:, s * ATT_SUB_KEYS:(s + 1) * ATT_SUB_KEYS]
    else:
        iw_ref[...] = iw.reshape(bb, tt, LANES)


def _mixer_in_call(x, mod, w_in_p, tabs, *, bb, tt, prompt_layout):
    bg, tg, d = x.shape
    row = lambda w: pl.BlockSpec((bb, tt, w), lambda i, j: (i, j, 0))
    sds = lambda w, dt: jax.ShapeDtypeStruct((bg, tg, w), dt)
    out_specs = [row(HALF)] * 8 + [row(LANES)]
    out_shape = [sds(HALF, BF16), sds(HALF, BF16), sds(HALF, BF16), sds(HALF, BF16), sds(HALF, BF16),
                 sds(HALF, F32), sds(HALF, F32), sds(HALF, BF16), sds(LANES, F32)]
    if prompt_layout:
        assert bb == 1 and tt == KEY_CHUNK
        out_specs += [pl.BlockSpec((1, IDX_HEADS, tt), lambda i, j: (i, 0, j)),
                      pl.BlockSpec((1, tt // ATT_SUB_KEYS, HALF, ATT_SUB_KEYS), lambda i, j: (i, j, 0, 0))]
        out_shape += [jax.ShapeDtypeStruct((bg, IDX_HEADS, tg), F32),
                      jax.ShapeDtypeStruct((bg, tg // ATT_SUB_KEYS, HALF, ATT_SUB_KEYS), BF16)]
    else:
        out_specs += [row(LANES)]
        out_shape += [sds(LANES, F32)]
    return pl.pallas_call(
        functools.partial(_mixer_in_kernel, prompt_layout=prompt_layout),
        grid=(bg // bb, tg // tt),
        in_specs=[row(d), pl.BlockSpec((bb, 9, d), lambda i, j: (i, 0, 0)),
                  _resident(w_in_p.shape),
                  pl.BlockSpec((N_TABS, tt, LANES), lambda i, j: (0, j, 0))],
        out_specs=out_specs,
        out_shape=out_shape,
        compiler_params=_params(("parallel", "parallel")),
        name="mixer_in",
    )(x, mod, w_in_p, tabs)


def _retention_tables(c):
    lg = jnp.log(1.0 - 2.0 ** (-5.0 - jnp.arange(RET_HEADS, dtype=F32)))
    n = jnp.arange(c, dtype=F32)
    rel = n[:, None] - n[None, :]
    dmat = jnp.where(rel[None] >= 0, jnp.exp(lg[:, None, None] * jnp.maximum(rel, 0.0)[None]), 0.0)
    cross = jnp.exp(lg[:, None] * (n[None, :] + 1.0))
    kdec = jnp.exp(lg[:, None] * (c - 1.0 - n)[None, :])
    sdec = jnp.exp(lg * c)
    bc = lambda v: jnp.broadcast_to(v[:, :, None], (RET_HEADS, c, LANES))
    return dmat, bc(cross), bc(kdec), jnp.broadcast_to(sdec[:, None, None], (RET_HEADS, 8, LANES))


def _retention_kernel(rq_ref, rk_ref, rv_ref, rg_ref, s0_ref, dmat_ref, cdec_ref, kdec_ref, sdec_ref,
                      o_ref, sout_ref, s_scr):
    @pl.when(pl.program_id(1) == 0)
    def _():
        s_scr[...] = s0_ref[0]

    for hd in range(RET_HEADS):
        sl = slice(RET_DK * hd, RET_DK * (hd + 1))
        q = rq_ref[0, :, sl]
        k = rk_ref[0, :, sl]
        v = rv_ref[0, :, sl]
        state = s_scr[hd]
        scores = lax.dot_general(q, k, _NT, preferred_element_type=F32) * dmat_ref[hd]
        inner = jnp.dot(scores.astype(BF16), v, preferred_element_type=F32)
        cross = jnp.dot(q, state.astype(BF16), preferred_element_type=F32) * cdec_ref[hd]
        kd = (k.astype(F32) * kdec_ref[hd]).astype(BF16)
        s_new = sdec_ref[hd, 0:1, :] * state + lax.dot_general(kd, v, _TN, preferred_element_type=F32)
        s_scr[hd] = s_new
        o = inner + cross
        mu = jnp.mean(o, -1, keepdims=True)
        dlt = o - mu
        var = jnp.mean(dlt * dlt, -1, keepdims=True)
        o_ref[0, :, sl] = (dlt * lax.rsqrt(var + LN_EPS) * _silu(rg_ref[0, :, sl].astype(F32))).astype(BF16)
    sout_ref[0] = s_scr[...]


def _retention_call(rq, rk, rv, rg, s0, *, chunk):
    bg, tg, _ = rq.shape
    dmat, cdec, kdec, sdec = _retention_tables(chunk)
    row = pl.BlockSpec((1, chunk, HALF), lambda b, c: (b, c, 0))
    st = pl.BlockSpec((1, RET_HEADS, RET_DK, RET_DK), lambda b, c: (b, 0, 0, 0))
    return pl.pallas_call(
        _retention_kernel,
        grid=(bg, tg // chunk),
        in_specs=[row, row, row, row, st, _resident(dmat.shape), _resident(cdec.shape), _resident(kdec.shape),
                  _resident(sdec.shape)],
        out_specs=[row, st],
        out_shape=[jax.ShapeDtypeStruct((bg, tg, HALF), BF16),
                   jax.ShapeDtypeStruct((bg, RET_HEADS, RET_DK, RET_DK), F32)],
        scratch_shapes=[pltpu.VMEM((RET_HEADS, RET_DK, RET_DK), F32)],
        compiler_params=_params(("parallel", "arbitrary")),
        name="retention",
    )(rq, rk, rv, rg, s0, dmat, cdec, kdec, sdec)


def _sortable_key(score):
    bits = pltpu.bitcast(score, I32)
    return bits ^ ((bits >> 31) & 0x7FFFFFFF)


def _kth_largest_key(count_ge, topk, like):
    lo = jnp.where(count_ge(jnp.zeros_like(like)) >= topk, 0, INT_MIN).astype(I32)

    def body(i, lo):
        cand = lo + jnp.left_shift(jnp.int32(1), 30 - i)
        return jnp.where(count_ge(cand) >= topk, cand, lo)

    return lax.fori_loop(0, 31, body, lo)


def _bit_planes(words):
    a = list(words)
    j, m = 16, 0x0000FFFF
    while j:
        k = 0
        while k < 32:
            t = (a[k] ^ (a[k + j] >> j)) & m
            a[k] = a[k] ^ t
            a[k + j] = a[k + j] ^ (t << j)
            k = (k + j + 1) & ~j
        j >>= 1
        m ^= (m << j) & 0xFFFFFFFF
    return a


def _kth_largest_from_planes(plane, alive, topk, like):
    def body(i, carry):
        alive, need, thr_u = carry
        ones = alive & plane(i)
        cnt = jnp.sum(lax.population_count(ones), axis=0, keepdims=True)
        ge = cnt >= need
        alive = jnp.where(ge, ones, alive ^ ones)
        need = jnp.where(ge, need, need - cnt)
        thr_u = jnp.where(ge, thr_u | jnp.left_shift(jnp.int32(1), 31 - i), thr_u)
        return alive, need, thr_u

    init = (alive, jnp.full_like(like, topk), jnp.zeros_like(like))
    alive, need, thr_u = lax.fori_loop(0, 32, body, init)
    return thr_u, need, jnp.sum(lax.population_count(alive), axis=0, keepdims=True)


def _tie_cut(count_tie_below, need, n_pos_bits, like):
    def body(i, j0):
        cand = j0 + jnp.left_shift(jnp.int32(1), n_pos_bits - 1 - i)
        return jnp.where(count_tie_below(cand) < need, cand, j0)

    return lax.fori_loop(0, n_pos_bits, body, jnp.zeros_like(like)) + 1


def _dsa_prompt_kernel(iq_ref, iwt_ref, aq_ref, ik2_ref, ak_ref, avt_ref, o_ref,
                       keys_scr, planes_scr, iqm_scr, aqm_scr, m_scr, acc_scr, outt_scr, *, topk, seq_len):
    qb = kc = KEY_CHUNK
    j = pl.program_id(1)
    nch = j + 1
    lane_q = lax.broadcasted_iota(I32, (1, qb), 1)
    limit = ((j * qb + lane_q) // MASK_CHUNK + 1) * MASK_CHUNK
    row_pos = lax.broadcasted_iota(I32, (kc, qb), 0)

    lane_head = lax.broadcasted_iota(I32, (qb, LANES), 1) // ATT_DH
    for hd in range(ATT_HEADS):
        col = slice(LANES * (hd // 2), LANES * (hd // 2 + 1))
        keep = lane_head == (hd % 2)
        iqm_scr[hd] = jnp.where(keep, iq_ref[0, :, col], jnp.zeros((), BF16))
        aqm_scr[hd] = jnp.where(keep, aq_ref[0, :, col], jnp.zeros((), BF16))

    iwt = iwt_ref[0]

    def index_chunk(c, carry):
        off = pl.multiple_of(c * kc, kc)
        ikc = ik2_ref[0, pl.ds(off, kc), :].astype(BF16)
        acc = jnp.zeros((kc, qb), F32)
        for hd in range(IDX_HEADS):
            s = lax.dot_general(ikc, iqm_scr[hd], _NT, preferred_element_type=F32)
            acc = acc + iwt[hd:hd + 1, :] * jnp.maximum(s, 0.0)
        key = jnp.where(row_pos + off < limit, _sortable_key(acc), INT_MIN)
        keys_scr[c] = key
        ukey = key ^ jnp.int32(INT_MIN)
        planes = _bit_planes([ukey[8 * r:8 * (r + 1), :] for r in range(32)])
        srow = pl.ds(pl.multiple_of(c * 8, 8), 8)
        for i in range(32):
            planes_scr[i, srow, :] = planes[i]
        return carry

    @pl.when(j == 0)
    def _():
        planes_scr[...] = jnp.zeros(planes_scr.shape, I32)

    lax.fori_loop(0, nch, index_chunk, 0)

    def count(pred):
        def body(c, acc):
            m = pred(keys_scr[c], row_pos + c * kc)
            return acc + jnp.sum(m.reshape(kc // 8, 8, qb), axis=0)
        acc = lax.fori_loop(0, nch, body, jnp.zeros((8, qb), I32))
        return jnp.sum(acc, axis=0, keepdims=True)

    one, zero = jnp.int32(1), jnp.int32(0)
    plane_chunk = lax.broadcasted_iota(I32, (planes_scr.shape[1], qb), 0) // 8
    alive = jnp.where(plane_chunk < nch, -1, 0).astype(I32)
    thr_u, need, n_eq = _kth_largest_from_planes(lambda i: planes_scr[i], alive, topk, lane_q)
    thr = thr_u ^ jnp.int32(INT_MIN)
    excess = jnp.where(thr == INT_MIN, 0, n_eq - need)
    take_all = jnp.where(thr == INT_MIN, 0, seq_len + 1).astype(I32)

    def tie_break():
        below = lambda jc: count(lambda k, p: jnp.where(k == thr, jnp.where(p < jc, one, zero), zero))
        cut = _tie_cut(below, need, max(1, int(math.ceil(math.log2(seq_len)))), lane_q)
        return jnp.where(excess > 0, cut, take_all)

    cut = lax.cond(jnp.max(excess) > 0, tie_break, lambda: take_all)

    m_scr[...] = jnp.full(m_scr.shape, NEG, F32)
    acc_scr[...] = jnp.zeros(acc_scr.shape, F32)
    sub = ATT_SUB_KEYS
    ones_rows = jnp.ones((ONES_ROWS, sub), BF16)
    sub_pos = lax.broadcasted_iota(I32, (sub, qb), 0)

    def attend(c, carry):
        off = pl.multiple_of(c * kc, kc)
        steps = [(s, hd) for s in range(kc // sub) for hd in range(ATT_HEADS)]
        biases = {}

        def logits(s, hd):
            if s not in biases:
                k = keys_scr[c, s * sub:(s + 1) * sub, :]
                pos = sub_pos + (c * kc + s * sub)
                biases[s] = jnp.where(k > thr, 0.0, jnp.where(k == thr, jnp.where(pos < cut, 0.0, NEG), NEG))
            col = slice(LANES * (hd // 2), LANES * (hd // 2 + 1))
            kcol = ak_ref[0, pl.ds(off + s * sub, sub), col].astype(BF16)
            return lax.dot_general(kcol, aqm_scr[hd], _NT, preferred_element_type=F32) + biases[s]

        ahead = [logits(*steps[i]) for i in range(ATT_LOOKAHEAD)]
        for i, (s, hd) in enumerate(steps):
            lg = ahead.pop(0)
            if i + ATT_LOOKAHEAD < len(steps):
                ahead.append(logits(*steps[i + ATT_LOOKAHEAD]))
            m_old = m_scr[hd]
            m_new = jnp.maximum(m_old, jnp.max(lg, axis=0, keepdims=True))
            m_scr[hd] = m_new
            p = jnp.exp2(lg - m_new).astype(BF16)
            vt = jnp.concatenate([avt_ref[0, c * (kc // sub) + s, ATT_DH * hd:ATT_DH * (hd + 1), :], ones_rows],
                                 axis=0)
            acc_scr[hd] = jnp.exp2(m_old - m_new) * acc_scr[hd] + jnp.dot(vt, p, preferred_element_type=F32)
        return carry

    lax.fori_loop(0, nch, attend, 0)
    for hd in range(ATT_HEADS):
        acc = acc_scr[hd]
        outt_scr[ATT_DH * hd:ATT_DH * (hd + 1), :] = acc[:ATT_DH] / acc[ATT_DH:ATT_DH + 1]
    o_ref[0] = outt_scr[...].T.astype(BF16)


def _dsa_prompt_call(iq, iwt, aq, ik2, ak, avt):
    bg, tg, _ = iq.shape
    qb = KEY_CHUNK
    nc = tg // qb
    topk = min(TOPK_MAX, tg // 4)
    assert topk <= qb
    blk = pl.BlockSpec((1, qb, HALF), lambda b, j: (b, j, 0))
    return pl.pallas_call(
        functools.partial(_dsa_prompt_kernel, topk=topk, seq_len=tg),
        grid=(bg, nc),
        in_specs=[blk,
                  pl.BlockSpec((1, IDX_HEADS, qb), lambda b, j: (b, 0, j)),
                  blk,
                  pl.BlockSpec((1, tg, LANES), lambda b, j: (b, 0, 0)),
                  pl.BlockSpec((1, tg, HALF), lambda b, j: (b, 0, 0)),
                  pl.BlockSpec((1, tg // ATT_SUB_KEYS, HALF, ATT_SUB_KEYS), lambda b, j: (b, 0, 0, 0))],
        out_specs=blk,
        out_shape=jax.ShapeDtypeStruct((bg, tg, HALF), BF16),
        scratch_shapes=[pltpu.VMEM((nc, qb, qb), I32), pltpu.VMEM((32, nc * qb // 32, qb), I32),
                        pltpu.VMEM((ATT_HEADS, qb, LANES), BF16), pltpu.VMEM((ATT_HEADS, qb, LANES), BF16),
                        pltpu.VMEM((ATT_HEADS, 1, qb), F32), pltpu.VMEM((ATT_HEADS, ATT_DH + ONES_ROWS, qb), F32),
                        pltpu.VMEM((HALF, qb), F32)],
        compiler_params=_params(("parallel", "arbitrary")),
        name="dsa_prompt",
    )(iq, iwt, aq, ik2, ak, avt)


def _dsa_sample_kernel(iq_ref, iw_ref, aq_ref, ik2n_ref, akn_ref, avn_ref, cik_ref, ck_ref, cv_ref, o_ref,
                       planes_scr, *, topk, past_len):
    nq = iq_ref.shape[1]
    n_new = nq
    total = past_len + n_new
    rows = ATT_HEADS * nq
    assert rows == 2 * LANES and past_len % 256 == 0 and n_new % 8 == 0 and n_new <= 256
    iq = iq_ref[0]
    aq = aq_ref[0]

    iq_rows = jnp.concatenate([iq[:, ATT_DH * hd:ATT_DH * (hd + 1)] for hd in range(IDX_HEADS)], axis=0)
    iw_t = iw_ref[0].T
    iw_lane = jnp.concatenate([iw_t[hd:hd + 1, :] for hd in range(IDX_HEADS)], axis=1)

    def index_scores(ik):
        s = lax.dot_general(ik.astype(BF16), iq_rows, _NT, preferred_element_type=F32)
        w = jnp.maximum(s, 0.0) * iw_lane
        r = w[:, :LANES] + w[:, LANES:]
        r = r + pltpu.roll(r, 2 * nq, 1)
        return r + pltpu.roll(r, nq, 1)

    lane = lax.broadcasted_iota(I32, (1, LANES), 1)
    qpos = past_len + lane % nq
    limit = (qpos // MASK_CHUNK + 1) * MASK_CHUNK
    pos_c = lax.broadcasted_iota(I32, (past_len, LANES), 0)
    pos_n = past_len + lax.broadcasted_iota(I32, (n_new, LANES), 0)
    key_c = jnp.where(pos_c < limit, _sortable_key(index_scores(cik_ref[0])), INT_MIN)
    key_n = jnp.where(pos_n < limit, _sortable_key(index_scores(ik2n_ref[0][:, :ATT_DH])), INT_MIN)

    n_groups = past_len // 256
    for g in range(n_groups + 1):
        if g < n_groups:
            words = [key_c[256 * g + 8 * r:256 * g + 8 * (r + 1), :] ^ jnp.int32(INT_MIN) for r in range(32)]
        else:
            words = [key_n[8 * r:8 * (r + 1), :] ^ jnp.int32(INT_MIN) if 8 * r < n_new
                     else jnp.zeros((8, LANES), I32) for r in range(32)]
        planes = _bit_planes(words)
        for i in range(32):
            planes_scr[i, 8 * g:8 * (g + 1), :] = planes[i]

    alive = jnp.full((planes_scr.shape[1], LANES), -1, I32)
    thr_u, need, n_eq = _kth_largest_from_planes(lambda i: planes_scr[i], alive, topk, lane)
    thr = thr_u ^ jnp.int32(INT_MIN)
    excess = jnp.where(thr == INT_MIN, 0, n_eq - need)
    take_all = jnp.where(thr == INT_MIN, 0, total + 1).astype(I32)

    one, zero = jnp.int32(1), jnp.int32(0)

    def tie_break():
        def below(jc):
            pred = lambda k, p: jnp.where(k == thr, jnp.where(p < jc, one, zero), zero)
            return (jnp.sum(pred(key_c, pos_c), axis=0, keepdims=True)
                    + jnp.sum(pred(key_n, pos_n), axis=0, keepdims=True))
        cut = _tie_cut(below, need, max(1, int(math.ceil(math.log2(total)))), lane)
        return jnp.where(excess > 0, cut, take_all)

    cut = lax.cond(jnp.max(excess) > 0, tie_break, lambda: take_all)

    def bias(k, pos):
        b = jnp.where(k > thr, 0.0, jnp.where(k == thr, jnp.where(pos < cut, 0.0, NEG), NEG))
        return jnp.concatenate([b, b], axis=1)

    lane_head = lax.broadcasted_iota(I32, (nq, HALF), 1) // ATT_DH
    q_bd = jnp.concatenate([jnp.where(lane_head == hd, aq, jnp.zeros((), BF16)) for hd in range(ATT_HEADS)], axis=0)

    def logits(k, b):
        return lax.dot_general(k.astype(BF16), q_bd, _NT, preferred_element_type=F32) + b

    lg_c = logits(ck_ref[0], bias(key_c, pos_c))
    lg_n = logits(akn_ref[0], bias(key_n, pos_n))
    m = jnp.maximum(jnp.max(lg_c, axis=0, keepdims=True), jnp.max(lg_n, axis=0, keepdims=True))
    p_c = jnp.exp2(lg_c - m).astype(BF16)
    p_n = jnp.exp2(lg_n - m).astype(BF16)

    def times_p(v_c, v_n):
        return (lax.dot_general(p_c, v_c, _TN, preferred_element_type=F32)
                + lax.dot_general(p_n, v_n, _TN, preferred_element_type=F32))

    res = times_p(cv_ref[0].astype(BF16), avn_ref[0].astype(BF16))
    denom = times_p(jnp.ones((past_len, LANES), BF16), jnp.ones((n_new, LANES), BF16))
    inv = 1.0 / denom
    res = (res * jnp.concatenate([inv] * (HALF // LANES), axis=1)).reshape(ATT_HEADS, nq, HALF)
    out = jnp.zeros((nq, HALF), F32)
    for hd in range(ATT_HEADS):
        out = out + jnp.where(lane_head == hd, res[hd], 0.0)
    o_ref[0] = out.astype(BF16)


def _dsa_sample_call(iq, iw, aq, ik2, ak, av, cache_ik, cache_k, cache_v):
    bg, nq, _ = iq.shape
    past_len = cache_k.shape[1]
    topk = min(TOPK_MAX, (past_len + nq) // 4)
    per_b = lambda n, w: pl.BlockSpec((1, n, w), lambda b: (b, 0, 0))
    return pl.pallas_call(
        functools.partial(_dsa_sample_kernel, topk=topk, past_len=past_len),
        grid=(bg,),
        in_specs=[per_b(nq, HALF), per_b(nq, LANES), per_b(nq, HALF), per_b(nq, LANES), per_b(nq, HALF),
                  per_b(nq, HALF), per_b(past_len, ATT_DH), per_b(past_len, HALF), per_b(past_len, HALF)],
        out_specs=per_b(nq, HALF),
        out_shape=jax.ShapeDtypeStruct((bg, nq, HALF), BF16),
        scratch_shapes=[pltpu.VMEM((32, 8 * (past_len // 256 + 1), LANES), I32)],
        compiler_params=_params(("parallel",)),
        name="dsa_sample",
    )(iq, iw, aq, ik2, ak, av, cache_ik, cache_k, cache_v)


def _pad_w_in(w_in):
    d = w_in.shape[0]
    main = 8 * HALF
    ik = w_in[:, main:main + ATT_DH]
    iw = w_in[:, main + ATT_DH:]
    pad = jnp.zeros((d, LANES - iw.shape[1]), w_in.dtype)
    return jnp.concatenate([w_in[:, :main], ik, ik, iw, pad], axis=1).astype(BF16)


def _layer(x_p, x_s, c_p, c_s, past_k, past_v, past_ik, ret_s0, depth,
           w_cond, b_cond, f1g, f1u, f1d, ln1_g, ln1_b, w_in, w_out, ln2_g, ln2_b, f2g, f2u, f2d, ln3_g, ln3_b):
    alpha = (2.0 * depth) ** 0.25
    bp, tp, d = x_p.shape
    bs, ts, _ = x_s.shape
    past_len = past_k.shape[1]

    mod = _cond_call(jnp.concatenate([c_p, c_s], axis=0), w_cond, b_cond).reshape(bp + bs, 9, d)
    mod_p, mod_s = mod[:bp], mod[bp:]

    w_in_p = _pad_w_in(w_in)
    w_out_b = w_out.astype(BF16)
    f1 = (f1g.astype(BF16), f1u.astype(BF16), f1d.astype(BF16), ln1_g, ln1_b)
    f2 = (f2g.astype(BF16), f2u.astype(BF16), f2d.astype(BF16), ln3_g, ln3_b)

    tt_p = min(512, tp)
    bb_s = min(bs, max(1, 512 // ts))
    geo_p = dict(bb=1, tt=tt_p)
    geo_s = dict(bb=bb_s, tt=ts)

    x1_p = _block_call(x_p, mod_p, *f1, alpha=alpha, mod_idx=(0, 1, 2), **geo_p)
    x1_s = _block_call(x_s, mod_s, *f1, alpha=alpha, mod_idx=(0, 1, 2), **geo_s)

    tabs_p = _rope_tables(jnp.arange(tp, dtype=I32))
    tabs_s = _rope_tables(past_len + jnp.arange(ts, dtype=I32))
    (rq, rk, rv, rg, aq, ak_p, av_p, iq, ik2_p, iwt, avt) = _mixer_in_call(
        x1_p, mod_p, w_in_p, tabs_p, bb=1, tt=KEY_CHUNK, prompt_layout=True)
    ret_chunk = min(256, tp)
    oret_p, state_p = _retention_call(rq, rk, rv, rg, jnp.zeros((bp, RET_HEADS, RET_DK, RET_DK), F32),
                                      chunk=ret_chunk)
    oatt_p = _dsa_prompt_call(iq, iwt, aq, ik2_p, ak_p, avt)

    (rq, rk, rv, rg, aq, ak_s, av_s, iq, ik2_s, iw) = _mixer_in_call(
        x1_s, mod_s, w_in_p, tabs_s, prompt_layout=False, **geo_s)
    oret_s, state_s = _retention_call(rq, rk, rv, rg, ret_s0, chunk=ts)
    oatt_s = _dsa_sample_call(iq, iw, aq, ik2_s, ak_s, av_s, past_ik,
                              past_k.reshape(bs, past_len, HALF), past_v.reshape(bs, past_len, HALF))

    y_p = _block_call(x1_p, mod_p, *f2, alpha=alpha, mod_idx=(6, 7, 8), mix=(oret_p, oatt_p, w_out_b, ln2_g, ln2_b),
                      **geo_p)
    y_s = _block_call(x1_s, mod_s, *f2, alpha=alpha, mod_idx=(6, 7, 8), mix=(oret_s, oatt_s, w_out_b, ln2_g, ln2_b),
                      **geo_s)

    heads = lambda a: a.reshape(a.shape[0], a.shape[1], ATT_HEADS, ATT_DH)
    st_p = (heads(ak_p), heads(av_p), ik2_p[..., :ATT_DH], state_p)
    st_s = (heads(ak_s), heads(av_s), ik2_s[..., :ATT_DH], state_s)
    return y_p, y_s, st_p, st_s


def kernel(x_prompt, x_sample, c_prompt, c_sample, cache_k, cache_v, cache_idx_k, state_ret, w_cond, b_cond, ffn1_w_gate, ffn1_w_up, ffn1_w_down, ln1_g, ln1_b, w_in, w_out, ln2_g, ln2_b, ffn2_w_gate, ffn2_w_up, ffn2_w_down, ln3_g, ln3_b):
    depth = w_cond.shape[0]
    y_p, y_s = x_prompt, x_sample
    st_p, st_s = [], []
    for l in range(depth):
        y_p, y_s, sp, ss = _layer(
            y_p, y_s, c_prompt, c_sample, cache_k[l], cache_v[l], cache_idx_k[l], state_ret[l], depth,
            w_cond[l], b_cond[l], ffn1_w_gate[l], ffn1_w_up[l], ffn1_w_down[l], ln1_g[l], ln1_b[l],
            w_in[l], w_out[l], ln2_g[l], ln2_b[l], ffn2_w_gate[l], ffn2_w_up[l], ffn2_w_down[l], ln3_g[l], ln3_b[l])
        st_p.append(sp)
        st_s.append(ss)
    stack = lambda sts, i: jnp.stack([s[i] for s in sts])
    return (y_p, y_s, stack(st_p, 0), stack(st_p, 1), stack(st_p, 2), stack(st_p, 3),
            stack(st_s, 0), stack(st_s, 1), stack(st_s, 2), stack(st_s, 3))
```

```python
import functools
import math

import jax
import jax.numpy as jnp
import numpy as np
from jax import lax
from jax.experimental import pallas as pl
from jax.experimental.pallas import tpu as pltpu

F32 = jnp.float32
BF16 = jnp.bfloat16
I32 = jnp.int32

RET_HEADS = 4
RET_DK = 128
RET_ROPE_BASE = 10000.0
ATT_HEADS = 8
ATT_DH = 64
ATT_ROT = 16
IDX_HEADS = 8
ROPE_THETA = 500000.0
MASK_CHUNK = 64
TOPK_MAX = 256
LN_EPS = 1e-5
NEG = -1e30
INT_MIN = -2 ** 31
HALF = 512
LANES = 128
N_TABS = 13
KEY_CHUNK = 256
ATT_SUB_KEYS = 128
ATT_LOOKAHEAD = 8
ONES_ROWS = 16
VMEM_LIMIT = 56 * 1024 * 1024
SAMPLE_VMEM_LIMIT = 60 * 1024 * 1024

_NT = (((1,), (1,)), ((), ()))
_TN = (((0,), (0,)), ((), ()))


def _params(sem):
    return pltpu.CompilerParams(dimension_semantics=sem, vmem_limit_bytes=VMEM_LIMIT)


def _layer_norm(v, g, b):
    mu = jnp.mean(v, -1, keepdims=True)
    d = v - mu
    var = jnp.mean(d * d, -1, keepdims=True)
    return d * lax.rsqrt(var + LN_EPS) * g + b


def _silu(a):
    return a * jax.nn.sigmoid(a)


def _resident(shape):
    nd = len(shape)
    return pl.BlockSpec(shape, lambda *_: (0,) * nd, pipeline_mode=pl.Buffered(1))


def _cond_kernel(c_ref, w_ref, b_ref, o_ref):
    s = _silu(c_ref[...])
    o_ref[...] = jnp.dot(s, w_ref[...], preferred_element_type=F32,
                         precision=lax.Precision.HIGHEST) + b_ref[...]


def _cond_call(c, w, b):
    n_rows, d = c.shape
    n = w.shape[1]
    tn = 1152
    return pl.pallas_call(
        _cond_kernel,
        grid=(n // tn,),
        in_specs=[pl.BlockSpec((n_rows, d), lambda i: (0, 0)),
                  pl.BlockSpec((d, tn), lambda i: (0, i)),
                  pl.BlockSpec((1, tn), lambda i: (0, i))],
        out_specs=pl.BlockSpec((n_rows, tn), lambda i: (0, i)),
        out_shape=jax.ShapeDtypeStruct((n_rows, n), F32),
        compiler_params=_params(("parallel",)),
        name="cond",
    )(c, w, b.reshape(1, n))


def _block_kernel(*refs, alpha, with_mix, mod_idx, n_chunks):
    if with_mix:
        (x_ref, mod_ref, oret_ref, oatt_ref, wout_ref, g2_ref, b2_ref,
         wg_ref, wu_ref, wd_ref, g_ref, b_ref, o_ref) = refs
    else:
        x_ref, mod_ref, wg_ref, wu_ref, wd_ref, g_ref, b_ref, o_ref = refs
    x = x_ref[...]
    bb, tt, d = x.shape
    rows = bb * tt
    mod = mod_ref[...]
    if with_mix:
        mixed = (jnp.dot(oret_ref[...].reshape(rows, HALF), wout_ref[:HALF, :], preferred_element_type=F32)
                 + jnp.dot(oatt_ref[...].reshape(rows, HALF), wout_ref[HALF:, :], preferred_element_type=F32))
        x = _layer_norm(alpha * x + (1.0 + mod[:, 5:6, :]) * mixed.reshape(bb, tt, d), g2_ref[...], b2_ref[...])
    i_sh, i_sc, i_gt = mod_idx
    h = (x * (1.0 + mod[:, i_sc:i_sc + 1, :]) + mod[:, i_sh:i_sh + 1, :]).reshape(rows, d).astype(BF16)
    dff = wg_ref.shape[1]
    ck = dff // n_chunks
    f = None
    for i in range(n_chunks):
        a = jnp.dot(h, wg_ref[:, i * ck:(i + 1) * ck], preferred_element_type=F32)
        u = jnp.dot(h, wu_ref[:, i * ck:(i + 1) * ck], preferred_element_type=F32)
        act = (_silu(a) * u).astype(BF16)
        part = jnp.dot(act, wd_ref[i * ck:(i + 1) * ck, :], preferred_element_type=F32)
        f = part if f is None else f + part
    y = alpha * x + (0.5 * (1.0 + mod[:, i_gt:i_gt + 1, :])) * f.reshape(bb, tt, d)
    o_ref[...] = _layer_norm(y, g_ref[...], b_ref[...])


def _block_call(x, mod, wg, wu, wd, g, b, *, alpha, mod_idx, bb, tt, mix=None):
    bg, tg, d = x.shape
    dff = wg.shape[1]
    row_spec = lambda w: pl.BlockSpec((bb, tt, w), lambda i, j: (i, j, 0))
    mod_spec = pl.BlockSpec((bb, 9, d), lambda i, j: (i, 0, 0))
    args = [x, mod]
    specs = [row_spec(d), mod_spec]
    if mix is not None:
        oret, oatt, wout, g2, b2 = mix
        args += [oret, oatt, wout, g2.reshape(1, d), b2.reshape(1, d)]
        specs += [row_spec(HALF), row_spec(HALF), _resident((d, d)), _resident((1, d)), _resident((1, d))]
    args += [wg, wu, wd, g.reshape(1, d), b.reshape(1, d)]
    specs += [_resident((d, dff)), _resident((d, dff)), _resident((dff, d)), _resident((1, d)), _resident((1, d))]
    return pl.pallas_call(
        functools.partial(_block_kernel, alpha=alpha, with_mix=mix is not None, mod_idx=mod_idx, n_chunks=2),
        grid=(bg // bb, tg // tt),
        in_specs=specs,
        out_specs=row_spec(d),
        out_shape=jax.ShapeDtypeStruct((bg, tg, d), F32),
        compiler_params=_params(("parallel", "parallel")),
        name="ffn_block" if mix is None else "mix_ffn_block",
    )(*args)


def _rope_tables(pos):
    posf = pos.astype(F32)[:, None]
    inv_r = 1.0 / (RET_ROPE_BASE ** (jnp.arange(0, RET_DK, 2, dtype=F32) / RET_DK))
    ang = posf * inv_r[None, :]
    cos_r, sin_r = jnp.cos(ang), jnp.sin(ang)
    cr = jnp.concatenate([cos_r, cos_r], -1)
    sr = jnp.concatenate([-sin_r, sin_r], -1)
    kscale = RET_DK ** -0.5
    inv_a = 1.0 / (ROPE_THETA ** (jnp.arange(0, ATT_ROT, 2, dtype=F32) / ATT_ROT))
    ang = posf * inv_a[None, :]
    cos_a, sin_a = jnp.cos(ang), jnp.sin(ang)
    t = pos.shape[0]
    half = ATT_ROT // 2
    rest = ATT_DH - ATT_ROT
    c64 = jnp.concatenate([cos_a, cos_a, jnp.ones((t, rest), F32)], -1)
    s1 = jnp.concatenate([-sin_a, jnp.zeros((t, ATT_DH - half), F32)], -1)
    s2 = jnp.concatenate([jnp.zeros((t, half), F32), sin_a, jnp.zeros((t, rest), F32)], -1)
    ca, s1a, s2a = (jnp.concatenate([v, v], -1) for v in (c64, s1, s2))
    iscale = ATT_DH ** -0.5
    qscale = ATT_DH ** -0.5 * math.log2(math.e)
    return jnp.stack([cr, sr, cr * kscale, sr * kscale,
                      ca * iscale, s1a * iscale, s2a * iscale, ca, s1a, s2a,
                      ca * qscale, s1a * qscale, s2a * qscale])


def _mixer_in_kernel(*refs, prompt_layout):
    x_ref, mod_ref, w_ref, tab_ref = refs[:4]
    outs = refs[4:]
    rq_ref, rk_ref, rv_ref, rg_ref, aq_ref, ak_ref, av_ref, iq_ref, ik2_ref, iw_ref = outs[:10]
    x = x_ref[...]
    bb, tt, d = x.shape
    rows = bb * tt
    mod = mod_ref[...]
    h = (x * (1.0 + mod[:, 4:5, :]) + mod[:, 3:4, :]).reshape(rows, d).astype(BF16)

    def seg(i, width=HALF):
        return jnp.dot(h, w_ref[:, HALF * i:HALF * i + width], preferred_element_type=F32)

    def tab(i):
        return tab_ref[i][None]

    def rope_ret_col(zc, ci, si):
        r = pltpu.roll(zc, 64, 1)
        return zc.reshape(bb, tt, LANES) * tab(ci) + r.reshape(bb, tt, LANES) * tab(si)

    def rope_att_col(zc, ci):
        r1 = pltpu.roll(zc, LANES - ATT_ROT // 2, 1)
        r2 = pltpu.roll(zc, ATT_ROT // 2, 1)
        return (zc.reshape(bb, tt, LANES) * tab(ci) + r1.reshape(bb, tt, LANES) * tab(ci + 1)
                + r2.reshape(bb, tt, LANES) * tab(ci + 2))

    def cols(z, fn):
        return jnp.concatenate([fn(z[:, LANES * g:LANES * (g + 1)]) for g in range(z.shape[1] // LANES)], axis=-1)

    rq_ref[...] = cols(seg(0), lambda zc: rope_ret_col(zc, 0, 1)).astype(BF16)
    rk_ref[...] = cols(seg(1), lambda zc: rope_ret_col(zc, 2, 3)).astype(BF16)
    rv_ref[...] = seg(2).reshape(bb, tt, HALF).astype(BF16)
    rg_ref[...] = seg(3).reshape(bb, tt, HALF).astype(BF16)
    aq_ref[...] = cols(seg(4), lambda zc: rope_att_col(zc, 10)).astype(BF16)
    ak_ref[...] = cols(seg(5), lambda zc: rope_att_col(zc, 7))
    av = seg(6)
    av_ref[...] = av.reshape(bb, tt, HALF)
    iq_ref[...] = cols(seg(7), lambda zc: rope_att_col(zc, 4)).astype(BF16)
    tail = seg(8, 2 * LANES)
    ik2_ref[...] = rope_att_col(tail[:, :LANES], 7)
    iw = tail[:, LANES:] * (IDX_HEADS ** -0.5)
    if prompt_layout:
        avt_ref = outs[10]
        iw_ref[0] = iw.T[:IDX_HEADS, :]
        avt = av.T.astype(BF16)
        for s in range(tt // ATT_SUB_KEYS):
            avt_ref[0, s] = avt[:, s * ATT_SUB_KEYS:(s + 1) * ATT_SUB_KEYS]
    else:
        iw_ref[...] = iw.reshape(bb, tt, LANES)


def _mixer_in_call(x, mod, w_in_p, tabs, *, bb, tt, prompt_layout):
    bg, tg, d = x.shape
    row = lambda w: pl.BlockSpec((bb, tt, w), lambda i, j: (i, j, 0))
    sds = lambda w, dt: jax.ShapeDtypeStruct((bg, tg, w), dt)
    out_specs = [row(HALF)] * 8 + [row(LANES)]
    out_shape = [sds(HALF, BF16), sds(HALF, BF16), sds(HALF, BF16), sds(HALF, BF16), sds(HALF, BF16),
                 sds(HALF, F32), sds(HALF, F32), sds(HALF, BF16), sds(LANES, F32)]
    if prompt_layout:
        assert bb == 1 and tt == KEY_CHUNK
        out_specs += [pl.BlockSpec((1, IDX_HEADS, tt), lambda i, j: (i, 0, j)),
                      pl.BlockSpec((1, tt // ATT_SUB_KEYS, HALF, ATT_SUB_KEYS), lambda i, j: (i, j, 0, 0))]
        out_shape += [jax.ShapeDtypeStruct((bg, IDX_HEADS, tg), F32),
                      jax.ShapeDtypeStruct((bg, tg // ATT_SUB_KEYS, HALF, ATT_SUB_KEYS), BF16)]
    else:
        out_specs += [row(LANES)]
        out_shape += [sds(LANES, F32)]
    return pl.pallas_call(
        functools.partial(_mixer_in_kernel, prompt_layout=prompt_layout),
        grid=(bg // bb, tg // tt),
        in_specs=[row(d), pl.BlockSpec((bb, 9, d), lambda i, j: (i, 0, 0)),
                  _resident(w_in_p.shape),
                  pl.BlockSpec((N_TABS, tt, LANES), lambda i, j: (0, j, 0))],
        out_specs=out_specs,
        out_shape=out_shape,
        compiler_params=_params(("parallel", "parallel")),
        name="mixer_in",
    )(x, mod, w_in_p, tabs)


def _retention_tables(c):
    lg = jnp.log(1.0 - 2.0 ** (-5.0 - jnp.arange(RET_HEADS, dtype=F32)))
    n = jnp.arange(c, dtype=F32)
    rel = n[:, None] - n[None, :]
    dmat = jnp.where(rel[None] >= 0, jnp.exp(lg[:, None, None] * jnp.maximum(rel, 0.0)[None]), 0.0)
    cross = jnp.exp(lg[:, None] * (n[None, :] + 1.0))
    kdec = jnp.exp(lg[:, None] * (c - 1.0 - n)[None, :])
    sdec = jnp.exp(lg * c)
    bc = lambda v: jnp.broadcast_to(v[:, :, None], (RET_HEADS, c, LANES))
    return dmat, bc(cross), bc(kdec), jnp.broadcast_to(sdec[:, None, None], (RET_HEADS, 8, LANES))


def _retention_kernel(rq_ref, rk_ref, rv_ref, rg_ref, s0_ref, dmat_ref, cdec_ref, kdec_ref, sdec_ref,
                      o_ref, sout_ref, s_scr):
    @pl.when(pl.program_id(1) == 0)
    def _():
        s_scr[...] = s0_ref[0]

    for hd in range(RET_HEADS):
        sl = slice(RET_DK * hd, RET_DK * (hd + 1))
        q = rq_ref[0, :, sl]
        k = rk_ref[0, :, sl]
        v = rv_ref[0, :, sl]
        state = s_scr[hd]
        scores = lax.dot_general(q, k, _NT, preferred_element_type=F32) * dmat_ref[hd]
        inner = jnp.dot(scores.astype(BF16), v, preferred_element_type=F32)
        cross = jnp.dot(q, state.astype(BF16), preferred_element_type=F32) * cdec_ref[hd]
        kd = (k.astype(F32) * kdec_ref[hd]).astype(BF16)
        s_new = sdec_ref[hd, 0:1, :] * state + lax.dot_general(kd, v, _TN, preferred_element_type=F32)
        s_scr[hd] = s_new
        o = inner + cross
        mu = jnp.mean(o, -1, keepdims=True)
        dlt = o - mu
        var = jnp.mean(dlt * dlt, -1, keepdims=True)
        o_ref[0, :, sl] = (dlt * lax.rsqrt(var + LN_EPS) * _silu(rg_ref[0, :, sl].astype(F32))).astype(BF16)
    sout_ref[0] = s_scr[...]


def _retention_call(rq, rk, rv, rg, s0, *, chunk):
    bg, tg, _ = rq.shape
    dmat, cdec, kdec, sdec = _retention_tables(chunk)
    row = pl.BlockSpec((1, chunk, HALF), lambda b, c: (b, c, 0))
    st = pl.BlockSpec((1, RET_HEADS, RET_DK, RET_DK), lambda b, c: (b, 0, 0, 0))
    return pl.pallas_call(
        _retention_kernel,
        grid=(bg, tg // chunk),
        in_specs=[row, row, row, row, st, _resident(dmat.shape), _resident(cdec.shape), _resident(kdec.shape),
                  _resident(sdec.shape)],
        out_specs=[row, st],
        out_shape=[jax.ShapeDtypeStruct((bg, tg, HALF), BF16),
                   jax.ShapeDtypeStruct((bg, RET_HEADS, RET_DK, RET_DK), F32)],
        scratch_shapes=[pltpu.VMEM((RET_HEADS, RET_DK, RET_DK), F32)],
        compiler_params=_params(("parallel", "arbitrary")),
        name="retention",
    )(rq, rk, rv, rg, s0, dmat, cdec, kdec, sdec)


def _sortable_key(score):
    bits = pltpu.bitcast(score, I32)
    return bits ^ ((bits >> 31) & 0x7FFFFFFF)


def _kth_largest_key(count_ge, topk, like):
    lo = jnp.where(count_ge(jnp.zeros_like(like)) >= topk, 0, INT_MIN).astype(I32)

    def body(i, lo):
        cand = lo + jnp.left_shift(jnp.int32(1), 30 - i)
        return jnp.where(count_ge(cand) >= topk, cand, lo)

    return lax.fori_loop(0, 31, body, lo)


def _bit_planes(words):
    a = list(words)
    j, m = 16, 0x0000FFFF
    while j:
        k = 0
        while k < 32:
            t = (a[k] ^ (a[k + j] >> j)) & m
            a[k] = a[k] ^ t
            a[k + j] = a[k + j] ^ (t << j)
            k = (k + j + 1) & ~j
        j >>= 1
        m ^= (m << j) & 0xFFFFFFFF
    return a


def _kth_largest_from_planes(plane, alive, topk, like):
    def body(i, carry):
        alive, need, thr_u = carry
        ones = alive & plane(i)
        cnt = jnp.sum(lax.population_count(ones), axis=0, keepdims=True)
        ge = cnt >= need
        alive = jnp.where(ge, ones, alive ^ ones)
        need = jnp.where(ge, need, need - cnt)
        thr_u = jnp.where(ge, thr_u | jnp.left_shift(jnp.int32(1), 31 - i), thr_u)
        return alive, need, thr_u

    init = (alive, jnp.full_like(like, topk), jnp.zeros_like(like))
    alive, need, thr_u = lax.fori_loop(0, 32, body, init)
    return thr_u, need, jnp.sum(lax.population_count(alive), axis=0, keepdims=True)


def _tie_cut(count_tie_below, need, n_pos_bits, like):
    def body(i, j0):
        cand = j0 + jnp.left_shift(jnp.int32(1), n_pos_bits - 1 - i)
        return jnp.where(count_tie_below(cand) < need, cand, j0)

    return lax.fori_loop(0, n_pos_bits, body, jnp.zeros_like(like)) + 1


def _dsa_prompt_kernel(iq_ref, iwt_ref, aq_ref, ik2_ref, ak_ref, avt_ref, o_ref,
                       keys_scr, planes_scr, iqm_scr, aqm_scr, m_scr, acc_scr, outt_scr, *, topk, seq_len):
    qb = kc = KEY_CHUNK
    j = pl.program_id(1)
    nch = j + 1
    lane_q = lax.broadcasted_iota(I32, (1, qb), 1)
    limit = ((j * qb + lane_q) // MASK_CHUNK + 1) * MASK_CHUNK
    row_pos = lax.broadcasted_iota(I32, (kc, qb), 0)

    lane_head = lax.broadcasted_iota(I32, (qb, LANES), 1) // ATT_DH
    for hd in range(ATT_HEADS):
        col = slice(LANES * (hd // 2), LANES * (hd // 2 + 1))
        keep = lane_head == (hd % 2)
        iqm_scr[hd] = jnp.where(keep, iq_ref[0, :, col], jnp.zeros((), BF16))
        aqm_scr[hd] = jnp.where(keep, aq_ref[0, :, col], jnp.zeros((), BF16))

    iwt = iwt_ref[0]

    def index_chunk(c, carry):
        off = pl.multiple_of(c * kc, kc)
        ikc = ik2_ref[0, pl.ds(off, kc), :].astype(BF16)
        acc = jnp.zeros((kc, qb), F32)
        for hd in range(IDX_HEADS):
            s = lax.dot_general(ikc, iqm_scr[hd], _NT, preferred_element_type=F32)
            acc = acc + iwt[hd:hd + 1, :] * jnp.maximum(s, 0.0)
        key = jnp.where(row_pos + off < limit, _sortable_key(acc), INT_MIN)
        keys_scr[c] = key
        ukey = key ^ jnp.int32(INT_MIN)
        planes = _bit_planes([ukey[8 * r:8 * (r + 1), :] for r in range(32)])
        srow = pl.ds(pl.multiple_of(c * 8, 8), 8)
        for i in range(32):
            planes_scr[i, srow, :] = planes[i]
        return carry

    @pl.when(j == 0)
    def _():
        planes_scr[...] = jnp.zeros(planes_scr.shape, I32)

    lax.fori_loop(0, nch, index_chunk, 0)

    def count(pred):
        def body(c, acc):
            m = pred(keys_scr[c], row_pos + c * kc)
            return acc + jnp.sum(m.reshape(kc // 8, 8, qb), axis=0)
        acc = lax.fori_loop(0, nch, body, jnp.zeros((8, qb), I32))
        return jnp.sum(acc, axis=0, keepdims=True)

    one, zero = jnp.int32(1), jnp.int32(0)
    plane_chunk = lax.broadcasted_iota(I32, (planes_scr.shape[1], qb), 0) // 8
    alive = jnp.where(plane_chunk < nch, -1, 0).astype(I32)
    thr_u, need, n_eq = _kth_largest_from_planes(lambda i: planes_scr[i], alive, topk, lane_q)
    thr = thr_u ^ jnp.int32(INT_MIN)
    excess = jnp.where(thr == INT_MIN, 0, n_eq - need)
    take_all = jnp.where(thr == INT_MIN, 0, seq_len + 1).astype(I32)

    def tie_break():
        below = lambda jc: count(lambda k, p: jnp.where(k == thr, jnp.where(p < jc, one, zero), zero))
        cut = _tie_cut(below, need, max(1, int(math.ceil(math.log2(seq_len)))), lane_q)
        return jnp.where(excess > 0, cut, take_all)

    cut = lax.cond(jnp.max(excess) > 0, tie_break, lambda: take_all)

    m_scr[...] = jnp.full(m_scr.shape, NEG, F32)
    acc_scr[...] = jnp.zeros(acc_scr.shape, F32)
    sub = ATT_SUB_KEYS
    ones_rows = jnp.ones((ONES_ROWS, sub), BF16)
    sub_pos = lax.broadcasted_iota(I32, (sub, qb), 0)

    def attend(c, carry):
        off = pl.multiple_of(c * kc, kc)
        steps = [(s, hd) for s in range(kc // sub) for hd in range(ATT_HEADS)]
        biases = {}

        def logits(s, hd):
            if s not in biases:
                k = keys_scr[c, s * sub:(s + 1) * sub, :]
                pos = sub_pos + (c * kc + s * sub)
                biases[s] = jnp.where(k > thr, 0.0, jnp.where(k == thr, jnp.where(pos < cut, 0.0, NEG), NEG))
            col = slice(LANES * (hd // 2), LANES * (hd // 2 + 1))
            kcol = ak_ref[0, pl.ds(off + s * sub, sub), col].astype(BF16)
            return lax.dot_general(kcol, aqm_scr[hd], _NT, preferred_element_type=F32) + biases[s]

        ahead = [logits(*steps[i]) for i in range(ATT_LOOKAHEAD)]
        for i, (s, hd) in enumerate(steps):
            lg = ahead.pop(0)
            if i + ATT_LOOKAHEAD < len(steps):
                ahead.append(logits(*steps[i + ATT_LOOKAHEAD]))
            m_old = m_scr[hd]
            m_new = jnp.maximum(m_old, jnp.max(lg, axis=0, keepdims=True))
            m_scr[hd] = m_new
            p = jnp.exp2(lg - m_new).astype(BF16)
            vt = jnp.concatenate([avt_ref[0, c * (kc // sub) + s, ATT_DH * hd:ATT_DH * (hd + 1), :], ones_rows],
                                 axis=0)
            acc_scr[hd] = jnp.exp2(m_old - m_new) * acc_scr[hd] + jnp.dot(vt, p, preferred_element_type=F32)
        return carry

    lax.fori_loop(0, nch, attend, 0)
    for hd in range(ATT_HEADS):
        acc = acc_scr[hd]
        outt_scr[ATT_DH * hd:ATT_DH * (hd + 1), :] = acc[:ATT_DH] / acc[ATT_DH:ATT_DH + 1]
    o_ref[0] = outt_scr[...].T.astype(BF16)


def _dsa_prompt_call(iq, iwt, aq, ik2, ak, avt):
    bg, tg, _ = iq.shape
    qb = KEY_CHUNK
    nc = tg // qb
    topk = min(TOPK_MAX, tg // 4)
    assert topk <= qb
    blk = pl.BlockSpec((1, qb, HALF), lambda b, j: (b, j, 0))
    return pl.pallas_call(
        functools.partial(_dsa_prompt_kernel, topk=topk, seq_len=tg),
        grid=(bg, nc),
        in_specs=[blk,
                  pl.BlockSpec((1, IDX_HEADS, qb), lambda b, j: (b, 0, j)),
                  blk,
                  pl.BlockSpec((1, tg, LANES), lambda b, j: (b, 0, 0)),
                  pl.BlockSpec((1, tg, HALF), lambda b, j: (b, 0, 0)),
                  pl.BlockSpec((1, tg // ATT_SUB_KEYS, HALF, ATT_SUB_KEYS), lambda b, j: (b, 0, 0, 0))],
        out_specs=blk,
        out_shape=jax.ShapeDtypeStruct((bg, tg, HALF), BF16),
        scratch_shapes=[pltpu.VMEM((nc, qb, qb), I32), pltpu.VMEM((32, nc * qb // 32, qb), I32),
                        pltpu.VMEM((ATT_HEADS, qb, LANES), BF16), pltpu.VMEM((ATT_HEADS, qb, LANES), BF16),
                        pltpu.VMEM((ATT_HEADS, 1, qb), F32), pltpu.VMEM((ATT_HEADS, ATT_DH + ONES_ROWS, qb), F32),
                        pltpu.VMEM((HALF, qb), F32)],
        compiler_params=_params(("parallel", "arbitrary")),
        name="dsa_prompt",
    )(iq, iwt, aq, ik2, ak, avt)


def _dsa_sample_kernel(iq_ref, iw_ref, aq_ref, ik2n_ref, akn_ref, avn_ref, cik_ref, ck_ref, cv_ref, o_ref,
                       planes_scr, *, topk, past_len):
    nq = iq_ref.shape[1]
    n_new = nq
    total = past_len + n_new
    rows = ATT_HEADS * nq
    assert rows == 2 * LANES and past_len % 256 == 0 and n_new % 8 == 0 and n_new <= 256
    iq = iq_ref[0]
    aq = aq_ref[0]

    iq_rows = jnp.concatenate([iq[:, ATT_DH * hd:ATT_DH * (hd + 1)] for hd in range(IDX_HEADS)], axis=0)
    iw_t = iw_ref[0].T
    iw_lane = jnp.concatenate([iw_t[hd:hd + 1, :] for hd in range(IDX_HEADS)], axis=1)

    def index_scores(ik):
        s = lax.dot_general(ik.astype(BF16), iq_rows, _NT, preferred_element_type=F32)
        w = jnp.maximum(s, 0.0) * iw_lane
        r = w[:, :LANES] + w[:, LANES:]
        r = r + pltpu.roll(r, 2 * nq, 1)
        return r + pltpu.roll(r, nq, 1)

    lane = lax.broadcasted_iota(I32, (1, LANES), 1)
    qpos = past_len + lane % nq
    limit = (qpos // MASK_CHUNK + 1) * MASK_CHUNK
    pos_c = lax.broadcasted_iota(I32, (past_len, LANES), 0)
    pos_n = past_len + lax.broadcasted_iota(I32, (n_new, LANES), 0)
    key_c = jnp.where(pos_c < limit, _sortable_key(index_scores(cik_ref[0])), INT_MIN)
    key_n = jnp.where(pos_n < limit, _sortable_key(index_scores(ik2n_ref[0][:, :ATT_DH])), INT_MIN)

    n_groups = past_len // 256
    for g in range(n_groups + 1):
        if g < n_groups:
            words = [key_c[256 * g + 8 * r:256 * g + 8 * (r + 1), :] ^ jnp.int32(INT_MIN) for r in range(32)]
        else:
            words = [key_n[8 * r:8 * (r + 1), :] ^ jnp.int32(INT_MIN) if 8 * r < n_new
                     else jnp.zeros((8, LANES), I32) for r in range(32)]
        planes = _bit_planes(words)
        for i in range(32):
            planes_scr[i, 8 * g:8 * (g + 1), :] = planes[i]

    alive = jnp.full((planes_scr.shape[1], LANES), -1, I32)
    thr_u, need, n_eq = _kth_largest_from_planes(lambda i: planes_scr[i], alive, topk, lane)
    thr = thr_u ^ jnp.int32(INT_MIN)
    excess = jnp.where(thr == INT_MIN, 0, n_eq - need)
    take_all = jnp.where(thr == INT_MIN, 0, total + 1).astype(I32)

    one, zero = jnp.int32(1), jnp.int32(0)

    def tie_break():
        def below(jc):
            pred = lambda k, p: jnp.where(k == thr, jnp.where(p < jc, one, zero), zero)
            return (jnp.sum(pred(key_c, pos_c), axis=0, keepdims=True)
                    + jnp.sum(pred(key_n, pos_n), axis=0, keepdims=True))
        cut = _tie_cut(below, need, max(1, int(math.ceil(math.log2(total)))), lane)
        return jnp.where(excess > 0, cut, take_all)

    cut = lax.cond(jnp.max(excess) > 0, tie_break, lambda: take_all)

    def bias(k, pos):
        b = jnp.where(k > thr, 0.0, jnp.where(k == thr, jnp.where(pos < cut, 0.0, NEG), NEG))
        return jnp.concatenate([b, b], axis=1)

    lane_head = lax.broadcasted_iota(I32, (nq, HALF), 1) // ATT_DH
    q_bd = jnp.concatenate([jnp.where(lane_head == hd, aq, jnp.zeros((), BF16)) for hd in range(ATT_HEADS)], axis=0)

    def cached_head(ref, hd):
        return ref[0, pl.ds(hd, past_len, stride=ATT_HEADS), :].astype(BF16)

    lg_c = bias(key_c, pos_c)
    zero_q = jnp.zeros((nq, ATT_DH), BF16)
    for hd in range(ATT_HEADS):
        q_rows = jnp.concatenate([aq[:, ATT_DH * hd:ATT_DH * (hd + 1)] if h2 == hd else zero_q
                                  for h2 in range(ATT_HEADS)], axis=0)
        lg_c = lg_c + lax.dot_general(cached_head(ck_ref, hd), q_rows, _NT, preferred_element_type=F32)
    lg_n = lax.dot_general(akn_ref[0].astype(BF16), q_bd, _NT, preferred_element_type=F32) + bias(key_n, pos_n)
    m = jnp.maximum(jnp.max(lg_c, axis=0, keepdims=True), jnp.max(lg_n, axis=0, keepdims=True))
    pt_c = jnp.exp2(lg_c - m).T.astype(BF16)
    pt_n = jnp.exp2(lg_n - m).T.astype(BF16)

    denom = (jnp.dot(pt_c, jnp.ones((past_len, LANES), BF16), preferred_element_type=F32)
             + jnp.dot(pt_n, jnp.ones((n_new, LANES), BF16), preferred_element_type=F32))
    inv = 1.0 / denom
    res_n = jnp.dot(pt_n, avn_ref[0].astype(BF16), preferred_element_type=F32)
    res_n = (res_n * jnp.concatenate([inv] * (HALF // LANES), axis=1)).reshape(ATT_HEADS, nq, HALF)
    out = jnp.zeros((nq, HALF), F32)
    for hd in range(ATT_HEADS):
        out = out + jnp.where(lane_head == hd, res_n[hd], 0.0)
    heads = []
    for hd in range(ATT_HEADS):
        r = jnp.dot(pt_c[nq * hd:nq * (hd + 1), :], cached_head(cv_ref, hd), preferred_element_type=F32)
        heads.append(r * inv[nq * hd:nq * (hd + 1), :ATT_DH])
    o_ref[0] = (out + jnp.concatenate(heads, axis=1)).astype(BF16)


def _dsa_sample_call(iq, iw, aq, ik2, ak, av, cache_ik, cache_k, cache_v):
    bg, nq, _ = iq.shape
    past_len = cache_ik.shape[1]
    assert cache_k.shape == cache_v.shape == (bg, past_len * ATT_HEADS, ATT_DH)
    topk = min(TOPK_MAX, (past_len + nq) // 4)
    per_b = lambda n, w: pl.BlockSpec((1, n, w), lambda b: (b, 0, 0))
    return pl.pallas_call(
        functools.partial(_dsa_sample_kernel, topk=topk, past_len=past_len),
        grid=(bg,),
        in_specs=[per_b(nq, HALF), per_b(nq, LANES), per_b(nq, HALF), per_b(nq, LANES), per_b(nq, HALF),
                  per_b(nq, HALF), per_b(past_len, ATT_DH), per_b(past_len * ATT_HEADS, ATT_DH),
                  per_b(past_len * ATT_HEADS, ATT_DH)],
        out_specs=per_b(nq, HALF),
        out_shape=jax.ShapeDtypeStruct((bg, nq, HALF), BF16),
        scratch_shapes=[pltpu.VMEM((32, 8 * (past_len // 256 + 1), LANES), I32)],
        compiler_params=pltpu.CompilerParams(dimension_semantics=("parallel",), vmem_limit_bytes=SAMPLE_VMEM_LIMIT),
        name="dsa_sample",
    )(iq, iw, aq, ik2, ak, av, cache_ik, cache_k, cache_v)


def _pad_w_in(w_in):
    d = w_in.shape[0]
    main = 8 * HALF
    ik = w_in[:, main:main + ATT_DH]
    iw = w_in[:, main + ATT_DH:]
    pad = jnp.zeros((d, LANES - iw.shape[1]), w_in.dtype)
    return jnp.concatenate([w_in[:, :main], ik, ik, iw, pad], axis=1).astype(BF16)


def _layer(x_p, x_s, c_p, c_s, past_k, past_v, past_ik, ret_s0, depth,
           w_cond, b_cond, f1g, f1u, f1d, ln1_g, ln1_b, w_in, w_out, ln2_g, ln2_b, f2g, f2u, f2d, ln3_g, ln3_b):
    alpha = (2.0 * depth) ** 0.25
    bp, tp, d = x_p.shape
    bs, ts, _ = x_s.shape
    past_len = past_k.shape[1]

    mod = _cond_call(jnp.concatenate([c_p, c_s], axis=0), w_cond, b_cond).reshape(bp + bs, 9, d)
    mod_p, mod_s = mod[:bp], mod[bp:]

    w_in_p = _pad_w_in(w_in)
    w_out_b = w_out.astype(BF16)
    f1 = (f1g.astype(BF16), f1u.astype(BF16), f1d.astype(BF16), ln1_g, ln1_b)
    f2 = (f2g.astype(BF16), f2u.astype(BF16), f2d.astype(BF16), ln3_g, ln3_b)

    tt_p = min(512, tp)
    bb_s = min(bs, max(1, 512 // ts))
    geo_p = dict(bb=1, tt=tt_p)
    geo_s = dict(bb=bb_s, tt=ts)

    x1_p = _block_call(x_p, mod_p, *f1, alpha=alpha, mod_idx=(0, 1, 2), **geo_p)
    x1_s = _block_call(x_s, mod_s, *f1, alpha=alpha, mod_idx=(0, 1, 2), **geo_s)

    tabs_p = _rope_tables(jnp.arange(tp, dtype=I32))
    tabs_s = _rope_tables(past_len + jnp.arange(ts, dtype=I32))
    (rq, rk, rv, rg, aq, ak_p, av_p, iq, ik2_p, iwt, avt) = _mixer_in_call(
        x1_p, mod_p, w_in_p, tabs_p, bb=1, tt=KEY_CHUNK, prompt_layout=True)
    ret_chunk = min(256, tp)
    oret_p, state_p = _retention_call(rq, rk, rv, rg, jnp.zeros((bp, RET_HEADS, RET_DK, RET_DK), F32),
                                      chunk=ret_chunk)
    oatt_p = _dsa_prompt_call(iq, iwt, aq, ik2_p, ak_p, avt)

    (rq, rk, rv, rg, aq, ak_s, av_s, iq, ik2_s, iw) = _mixer_in_call(
        x1_s, mod_s, w_in_p, tabs_s, prompt_layout=False, **geo_s)
    oret_s, state_s = _retention_call(rq, rk, rv, rg, ret_s0, chunk=ts)
    oatt_s = _dsa_sample_call(iq, iw, aq, ik2_s, ak_s, av_s, past_ik,
                              past_k.reshape(bs, past_len * ATT_HEADS, ATT_DH),
                              past_v.reshape(bs, past_len * ATT_HEADS, ATT_DH))

    y_p = _block_call(x1_p, mod_p, *f2, alpha=alpha, mod_idx=(6, 7, 8), mix=(oret_p, oatt_p, w_out_b, ln2_g, ln2_b),
                      **geo_p)
    y_s = _block_call(x1_s, mod_s, *f2, alpha=alpha, mod_idx=(6, 7, 8), mix=(oret_s, oatt_s, w_out_b, ln2_g, ln2_b),
                      **geo_s)

    heads = lambda a: a.reshape(a.shape[0], a.shape[1], ATT_HEADS, ATT_DH)
    st_p = (heads(ak_p), heads(av_p), ik2_p[..., :ATT_DH], state_p)
    st_s = (heads(ak_s), heads(av_s), ik2_s[..., :ATT_DH], state_s)
    return y_p, y_s, st_p, st_s


def kernel(x_prompt, x_sample, c_prompt, c_sample, cache_k, cache_v, cache_idx_k, state_ret, w_cond, b_cond, ffn1_w_gate, ffn1_w_up, ffn1_w_down, ln1_g, ln1_b, w_in, w_out, ln2_g, ln2_b, ffn2_w_gate, ffn2_w_up, ffn2_w_down, ln3_g, ln3_b):
    depth = w_cond.shape[0]
    y_p, y_s = x_prompt, x_sample
    st_p, st_s = [], []
    for l in range(depth):
        y_p, y_s, sp, ss = _layer(
            y_p, y_s, c_prompt, c_sample, cache_k[l], cache_v[l], cache_idx_k[l], state_ret[l], depth,
            w_cond[l], b_cond[l], ffn1_w_gate[l], ffn1_w_up[l], ffn1_w_down[l], ln1_g[l], ln1_b[l],
            w_in[l], w_out[l], ln2_g[l], ln2_b[l], ffn2_w_gate[l], ffn2_w_up[l], ffn2_w_down[l], ln3_g[l], ln3_b[l])
        st_p.append(sp)
        st_s.append(ss)
    stack = lambda sts, i: jnp.stack([s[i] for s in sts])
    return (y_p, y_s, stack(st_p, 0), stack(st_p, 1), stack(st_p, 2), stack(st_p, 3),
            stack(st_s, 0), stack(st_s, 1), stack(st_s, 2), stack(st_s, 3))
```

```python
import functools
import math

import jax
import jax.numpy as jnp
import numpy as np
from jax import lax
from jax.experimental import pallas as pl
from jax.experimental.pallas import tpu as pltpu

F32 = jnp.float32
BF16 = jnp.bfloat16
I32 = jnp.int32

RET_HEADS = 4
RET_DK = 128
RET_ROPE_BASE = 10000.0
ATT_HEADS = 8
ATT_DH = 64
ATT_ROT = 16
IDX_HEADS = 8
ROPE_THETA = 500000.0
MASK_CHUNK = 64
TOPK_MAX = 256
LN_EPS = 1e-5
NEG = -1e30
INT_MIN = -2 ** 31
HALF = 512
LANES = 128
N_TABS = 13
KEY_CHUNK = 256
ATT_SUB_KEYS = 128
ATT_LOOKAHEAD = 8
ONES_ROWS = 16
VMEM_LIMIT = 56 * 1024 * 1024

_NT = (((1,), (1,)), ((), ()))
_TN = (((0,), (0,)), ((), ()))


def _params(sem):
    return pltpu.CompilerParams(dimension_semantics=sem, vmem_limit_bytes=VMEM_LIMIT)


def _layer_norm(v, g, b):
    mu = jnp.mean(v, -1, keepdims=True)
    d = v - mu
    var = jnp.mean(d * d, -1, keepdims=True)
    return d * lax.rsqrt(var + LN_EPS) * g + b


def _silu(a):
    return a * jax.nn.sigmoid(a)


def _resident(shape):
    nd = len(shape)
    return pl.BlockSpec(shape, lambda *_: (0,) * nd, pipeline_mode=pl.Buffered(1))


def _cond_kernel(c_ref, w_ref, b_ref, o_ref):
    s = _silu(c_ref[...])
    o_ref[...] = jnp.dot(s, w_ref[...], preferred_element_type=F32,
                         precision=lax.Precision.HIGHEST) + b_ref[...]


def _cond_call(c, w, b):
    n_rows, d = c.shape
    n = w.shape[1]
    tn = 1152
    return pl.pallas_call(
        _cond_kernel,
        grid=(n // tn,),
        in_specs=[pl.BlockSpec((n_rows, d), lambda i: (0, 0)),
                  pl.BlockSpec((d, tn), lambda i: (0, i)),
                  pl.BlockSpec((1, tn), lambda i: (0, i))],
        out_specs=pl.BlockSpec((n_rows, tn), lambda i: (0, i)),
        out_shape=jax.ShapeDtypeStruct((n_rows, n), F32),
        compiler_params=_params(("parallel",)),
        name="cond",
    )(c, w, b.reshape(1, n))


def _block_kernel(*refs, alpha, with_mix, mod_idx, n_chunks):
    if with_mix:
        (x_ref, mod_ref, oret_ref, oatt_ref, wout_ref, g2_ref, b2_ref,
         wg_ref, wu_ref, wd_ref, g_ref, b_ref, o_ref) = refs
    else:
        x_ref, mod_ref, wg_ref, wu_ref, wd_ref, g_ref, b_ref, o_ref = refs
    x = x_ref[...]
    bb, tt, d = x.shape
    rows = bb * tt
    mod = mod_ref[...]
    if with_mix:
        mixed = (jnp.dot(oret_ref[...].reshape(rows, HALF), wout_ref[:HALF, :], preferred_element_type=F32)
                 + jnp.dot(oatt_ref[...].reshape(rows, HALF), wout_ref[HALF:, :], preferred_element_type=F32))
        x = _layer_norm(alpha * x + (1.0 + mod[:, 5:6, :]) * mixed.reshape(bb, tt, d), g2_ref[...], b2_ref[...])
    i_sh, i_sc, i_gt = mod_idx
    h = (x * (1.0 + mod[:, i_sc:i_sc + 1, :]) + mod[:, i_sh:i_sh + 1, :]).reshape(rows, d).astype(BF16)
    dff = wg_ref.shape[1]
    ck = dff // n_chunks
    f = None
    for i in range(n_chunks):
        a = jnp.dot(h, wg_ref[:, i * ck:(i + 1) * ck], preferred_element_type=F32)
        u = jnp.dot(h, wu_ref[:, i * ck:(i + 1) * ck], preferred_element_type=F32)
        act = (_silu(a) * u).astype(BF16)
        part = jnp.dot(act, wd_ref[i * ck:(i + 1) * ck, :], preferred_element_type=F32)
        f = part if f is None else f + part
    y = alpha * x + (0.5 * (1.0 + mod[:, i_gt:i_gt + 1, :])) * f.reshape(bb, tt, d)
    o_ref[...] = _layer_norm(y, g_ref[...], b_ref[...])


def _block_call(x, mod, wg, wu, wd, g, b, *, alpha, mod_idx, bb, tt, mix=None):
    bg, tg, d = x.shape
    dff = wg.shape[1]
    row_spec = lambda w: pl.BlockSpec((bb, tt, w), lambda i, j: (i, j, 0))
    mod_spec = pl.BlockSpec((bb, 9, d), lambda i, j: (i, 0, 0))
    args = [x, mod]
    specs = [row_spec(d), mod_spec]
    if mix is not None:
        oret, oatt, wout, g2, b2 = mix
        args += [oret, oatt, wout, g2.reshape(1, d), b2.reshape(1, d)]
        specs += [row_spec(HALF), row_spec(HALF), _resident((d, d)), _resident((1, d)), _resident((1, d))]
    args += [wg, wu, wd, g.reshape(1, d), b.reshape(1, d)]
    specs += [_resident((d, dff)), _resident((d, dff)), _resident((dff, d)), _resident((1, d)), _resident((1, d))]
    return pl.pallas_call(
        functools.partial(_block_kernel, alpha=alpha, with_mix=mix is not None, mod_idx=mod_idx, n_chunks=2),
        grid=(bg // bb, tg // tt),
        in_specs=specs,
        out_specs=row_spec(d),
        out_shape=jax.ShapeDtypeStruct((bg, tg, d), F32),
        compiler_params=_params(("parallel", "parallel")),
        name="ffn_block" if mix is None else "mix_ffn_block",
    )(*args)


def _rope_tables(pos):
    posf = pos.astype(F32)[:, None]
    inv_r = 1.0 / (RET_ROPE_BASE ** (jnp.arange(0, RET_DK, 2, dtype=F32) / RET_DK))
    ang = posf * inv_r[None, :]
    cos_r, sin_r = jnp.cos(ang), jnp.sin(ang)
    cr = jnp.concatenate([cos_r, cos_r], -1)
    sr = jnp.concatenate([-sin_r, sin_r], -1)
    kscale = RET_DK ** -0.5
    inv_a = 1.0 / (ROPE_THETA ** (jnp.arange(0, ATT_ROT, 2, dtype=F32) / ATT_ROT))
    ang = posf * inv_a[None, :]
    cos_a, sin_a = jnp.cos(ang), jnp.sin(ang)
    t = pos.shape[0]
    half = ATT_ROT // 2
    rest = ATT_DH - ATT_ROT
    c64 = jnp.concatenate([cos_a, cos_a, jnp.ones((t, rest), F32)], -1)
    s1 = jnp.concatenate([-sin_a, jnp.zeros((t, ATT_DH - half), F32)], -1)
    s2 = jnp.concatenate([jnp.zeros((t, half), F32), sin_a, jnp.zeros((t, rest), F32)], -1)
    ca, s1a, s2a = (jnp.concatenate([v, v], -1) for v in (c64, s1, s2))
    iscale = ATT_DH ** -0.5
    qscale = ATT_DH ** -0.5 * math.log2(math.e)
    return jnp.stack([cr, sr, cr * kscale, sr * kscale,
                      ca * iscale, s1a * iscale, s2a * iscale, ca, s1a, s2a,
                      ca * qscale, s1a * qscale, s2a * qscale])


def _mixer_in_kernel(*refs, prompt_layout):
    x_ref, mod_ref, w_ref, tab_ref = refs[:4]
    outs = refs[4:]
    rq_ref, rk_ref, rv_ref, rg_ref, aq_ref, iq_ref, ak_ref, ik2_ref, iw_ref = outs[:9]
    x = x_ref[...]
    bb, tt, d = x.shape
    rows = bb * tt
    mod = mod_ref[...]
    h = (x * (1.0 + mod[:, 4:5, :]) + mod[:, 3:4, :]).reshape(rows, d).astype(BF16)

    def seg(i, width=HALF):
        return jnp.dot(h, w_ref[:, HALF * i:HALF * i + width], preferred_element_type=F32)

    def tab(i):
        return tab_ref[i][None]

    def rope_ret_col(zc, ci, si):
        r = pltpu.roll(zc, 64, 1)
        return zc.reshape(bb, tt, LANES) * tab(ci) + r.reshape(bb, tt, LANES) * tab(si)

    def rope_att_col(zc, ci):
        r1 = pltpu.roll(zc, LANES - ATT_ROT // 2, 1)
        r2 = pltpu.roll(zc, ATT_ROT // 2, 1)
        return (zc.reshape(bb, tt, LANES) * tab(ci) + r1.reshape(bb, tt, LANES) * tab(ci + 1)
                + r2.reshape(bb, tt, LANES) * tab(ci + 2))

    def cols(z, fn):
        return jnp.concatenate([fn(z[:, LANES * g:LANES * (g + 1)]) for g in range(z.shape[1] // LANES)], axis=-1)

    rq_ref[...] = cols(seg(0), lambda zc: rope_ret_col(zc, 0, 1)).astype(BF16)
    rk_ref[...] = cols(seg(1), lambda zc: rope_ret_col(zc, 2, 3)).astype(BF16)
    rv_ref[...] = seg(2).reshape(bb, tt, HALF).astype(BF16)
    rg_ref[...] = seg(3).reshape(bb, tt, HALF).astype(BF16)
    aq_ref[...] = cols(seg(4), lambda zc: rope_att_col(zc, 10)).astype(BF16)
    ak = cols(seg(5), lambda zc: rope_att_col(zc, 7))
    av = seg(6)
    iq_ref[...] = cols(seg(7), lambda zc: rope_att_col(zc, 4)).astype(BF16)
    tail = seg(8, 2 * LANES)
    ik2 = rope_att_col(tail[:, :LANES], 7)
    iw = tail[:, LANES:] * (IDX_HEADS ** -0.5)
    if prompt_layout:
        akt_ref, avt_ref, ikt_ref, avc_ref = outs[9:]
        ak_ref[...] = ak.astype(BF16)
        ik2_ref[...] = ik2.astype(BF16)
        akt_ref[0] = ak.reshape(rows, HALF).T
        avt = av.T
        avt_ref[0] = avt
        ikt_ref[0] = ik2.reshape(rows, LANES).T[:ATT_DH, :]
        iw_ref[0] = iw.T[:IDX_HEADS, :]
        for s in range(tt // ATT_SUB_KEYS):
            avc_ref[0, s] = avt[:, s * ATT_SUB_KEYS:(s + 1) * ATT_SUB_KEYS].astype(BF16)
    else:
        av_ref, = outs[9:]
        ak_ref[...] = ak
        av_ref[...] = av.reshape(bb, tt, HALF)
        ik2_ref[...] = ik2
        iw_ref[...] = iw.reshape(bb, tt, LANES)


def _mixer_in_call(x, mod, w_in_p, tabs, *, bb, tt, prompt_layout):
    bg, tg, d = x.shape
    row = lambda w: pl.BlockSpec((bb, tt, w), lambda i, j: (i, j, 0))
    sds = lambda w, dt: jax.ShapeDtypeStruct((bg, tg, w), dt)
    out_specs = [row(HALF)] * 7 + [row(LANES)]
    out_shape = [sds(HALF, BF16)] * 6
    if prompt_layout:
        assert bb == 1 and tt == KEY_CHUNK
        feat = lambda n: pl.BlockSpec((1, n, tt), lambda i, j: (i, 0, j))
        out_specs += [feat(IDX_HEADS), feat(HALF), feat(HALF), feat(ATT_DH),
                      pl.BlockSpec((1, tt // ATT_SUB_KEYS, HALF, ATT_SUB_KEYS), lambda i, j: (i, j, 0, 0))]
        out_shape += [sds(HALF, BF16), sds(LANES, BF16)]
        out_shape += [jax.ShapeDtypeStruct((bg, n, tg), F32) for n in (IDX_HEADS, HALF, HALF, ATT_DH)]
        out_shape += [jax.ShapeDtypeStruct((bg, tg // ATT_SUB_KEYS, HALF, ATT_SUB_KEYS), BF16)]
    else:
        out_specs += [row(LANES), row(HALF)]
        out_shape += [sds(HALF, F32), sds(LANES, F32), sds(LANES, F32), sds(HALF, F32)]
    return pl.pallas_call(
        functools.partial(_mixer_in_kernel, prompt_layout=prompt_layout),
        grid=(bg // bb, tg // tt),
        in_specs=[row(d), pl.BlockSpec((bb, 9, d), lambda i, j: (i, 0, 0)),
                  _resident(w_in_p.shape),
                  pl.BlockSpec((N_TABS, tt, LANES), lambda i, j: (0, j, 0))],
        out_specs=out_specs,
        out_shape=out_shape,
        compiler_params=_params(("parallel", "parallel")),
        name="mixer_in",
    )(x, mod, w_in_p, tabs)


def _retention_tables(c):
    lg = jnp.log(1.0 - 2.0 ** (-5.0 - jnp.arange(RET_HEADS, dtype=F32)))
    n = jnp.arange(c, dtype=F32)
    rel = n[:, None] - n[None, :]
    dmat = jnp.where(rel[None] >= 0, jnp.exp(lg[:, None, None] * jnp.maximum(rel, 0.0)[None]), 0.0)
    cross = jnp.exp(lg[:, None] * (n[None, :] + 1.0))
    kdec = jnp.exp(lg[:, None] * (c - 1.0 - n)[None, :])
    sdec = jnp.exp(lg * c)
    bc = lambda v: jnp.broadcast_to(v[:, :, None], (RET_HEADS, c, LANES))
    return dmat, bc(cross), bc(kdec), jnp.broadcast_to(sdec[:, None, None], (RET_HEADS, 8, LANES))


def _retention_kernel(rq_ref, rk_ref, rv_ref, rg_ref, s0_ref, dmat_ref, cdec_ref, kdec_ref, sdec_ref,
                      o_ref, sout_ref, s_scr):
    @pl.when(pl.program_id(1) == 0)
    def _():
        s_scr[...] = s0_ref[0]

    for hd in range(RET_HEADS):
        sl = slice(RET_DK * hd, RET_DK * (hd + 1))
        q = rq_ref[0, :, sl]
        k = rk_ref[0, :, sl]
        v = rv_ref[0, :, sl]
        state = s_scr[hd]
        scores = lax.dot_general(q, k, _NT, preferred_element_type=F32) * dmat_ref[hd]
        inner = jnp.dot(scores.astype(BF16), v, preferred_element_type=F32)
        cross = jnp.dot(q, state.astype(BF16), preferred_element_type=F32) * cdec_ref[hd]
        kd = (k.astype(F32) * kdec_ref[hd]).astype(BF16)
        s_new = sdec_ref[hd, 0:1, :] * state + lax.dot_general(kd, v, _TN, preferred_element_type=F32)
        s_scr[hd] = s_new
        o = inner + cross
        mu = jnp.mean(o, -1, keepdims=True)
        dlt = o - mu
        var = jnp.mean(dlt * dlt, -1, keepdims=True)
        o_ref[0, :, sl] = (dlt * lax.rsqrt(var + LN_EPS) * _silu(rg_ref[0, :, sl].astype(F32))).astype(BF16)
    sout_ref[0] = s_scr[...]


def _retention_call(rq, rk, rv, rg, s0, *, chunk):
    bg, tg, _ = rq.shape
    dmat, cdec, kdec, sdec = _retention_tables(chunk)
    row = pl.BlockSpec((1, chunk, HALF), lambda b, c: (b, c, 0))
    st = pl.BlockSpec((1, RET_HEADS, RET_DK, RET_DK), lambda b, c: (b, 0, 0, 0))
    return pl.pallas_call(
        _retention_kernel,
        grid=(bg, tg // chunk),
        in_specs=[row, row, row, row, st, _resident(dmat.shape), _resident(cdec.shape), _resident(kdec.shape),
                  _resident(sdec.shape)],
        out_specs=[row, st],
        out_shape=[jax.ShapeDtypeStruct((bg, tg, HALF), BF16),
                   jax.ShapeDtypeStruct((bg, RET_HEADS, RET_DK, RET_DK), F32)],
        scratch_shapes=[pltpu.VMEM((RET_HEADS, RET_DK, RET_DK), F32)],
        compiler_params=_params(("parallel", "arbitrary")),
        name="retention",
    )(rq, rk, rv, rg, s0, dmat, cdec, kdec, sdec)


def _sortable_key(score):
    bits = pltpu.bitcast(score, I32)
    return bits ^ ((bits >> 31) & 0x7FFFFFFF)


def _kth_largest_key(count_ge, topk, like):
    lo = jnp.where(count_ge(jnp.zeros_like(like)) >= topk, 0, INT_MIN).astype(I32)

    def body(i, lo):
        cand = lo + jnp.left_shift(jnp.int32(1), 30 - i)
        return jnp.where(count_ge(cand) >= topk, cand, lo)

    return lax.fori_loop(0, 31, body, lo)


def _bit_planes(words):
    a = list(words)
    j, m = 16, 0x0000FFFF
    while j:
        k = 0
        while k < 32:
            t = (a[k] ^ (a[k + j] >> j)) & m
            a[k] = a[k] ^ t
            a[k + j] = a[k + j] ^ (t << j)
            k = (k + j + 1) & ~j
        j >>= 1
        m ^= (m << j) & 0xFFFFFFFF
    return a


def _kth_largest_from_planes(plane, alive, topk, like):
    def body(i, carry):
        alive, need, thr_u = carry
        ones = alive & plane(i)
        cnt = jnp.sum(lax.population_count(ones), axis=0, keepdims=True)
        ge = cnt >= need
        alive = jnp.where(ge, ones, alive ^ ones)
        need = jnp.where(ge, need, need - cnt)
        thr_u = jnp.where(ge, thr_u | jnp.left_shift(jnp.int32(1), 31 - i), thr_u)
        return alive, need, thr_u

    init = (alive, jnp.full_like(like, topk), jnp.zeros_like(like))
    alive, need, thr_u = lax.fori_loop(0, 32, body, init)
    return thr_u, need, jnp.sum(lax.population_count(alive), axis=0, keepdims=True)


def _tie_cut(count_tie_below, need, n_pos_bits, like):
    def body(i, j0):
        cand = j0 + jnp.left_shift(jnp.int32(1), n_pos_bits - 1 - i)
        return jnp.where(count_tie_below(cand) < need, cand, j0)

    return lax.fori_loop(0, n_pos_bits, body, jnp.zeros_like(like)) + 1


def _dsa_prompt_kernel(iq_ref, iwt_ref, aq_ref, ik2_ref, ak_ref, avt_ref, o_ref,
                       keys_scr, planes_scr, iqm_scr, aqm_scr, m_scr, acc_scr, outt_scr, *, topk, seq_len):
    qb = kc = KEY_CHUNK
    j = pl.program_id(1)
    nch = j + 1
    lane_q = lax.broadcasted_iota(I32, (1, qb), 1)
    limit = ((j * qb + lane_q) // MASK_CHUNK + 1) * MASK_CHUNK
    row_pos = lax.broadcasted_iota(I32, (kc, qb), 0)

    lane_head = lax.broadcasted_iota(I32, (qb, LANES), 1) // ATT_DH
    for hd in range(ATT_HEADS):
        col = slice(LANES * (hd // 2), LANES * (hd // 2 + 1))
        keep = lane_head == (hd % 2)
        iqm_scr[hd] = jnp.where(keep, iq_ref[0, :, col], jnp.zeros((), BF16))
        aqm_scr[hd] = jnp.where(keep, aq_ref[0, :, col], jnp.zeros((), BF16))

    iwt = iwt_ref[0]

    def index_chunk(c, carry):
        off = pl.multiple_of(c * kc, kc)
        ikc = ik2_ref[0, pl.ds(off, kc), :]
        acc = jnp.zeros((kc, qb), F32)
        for hd in range(IDX_HEADS):
            s = lax.dot_general(ikc, iqm_scr[hd], _NT, preferred_element_type=F32)
            acc = acc + iwt[hd:hd + 1, :] * jnp.maximum(s, 0.0)
        key = jnp.where(row_pos + off < limit, _sortable_key(acc), INT_MIN)
        keys_scr[c] = key
        ukey = key ^ jnp.int32(INT_MIN)
        planes = _bit_planes([ukey[8 * r:8 * (r + 1), :] for r in range(32)])
        srow = pl.ds(pl.multiple_of(c * 8, 8), 8)
        for i in range(32):
            planes_scr[i, srow, :] = planes[i]
        return carry

    @pl.when(j == 0)
    def _():
        planes_scr[...] = jnp.zeros(planes_scr.shape, I32)

    lax.fori_loop(0, nch, index_chunk, 0)

    def count(pred):
        def body(c, acc):
            m = pred(keys_scr[c], row_pos + c * kc)
            return acc + jnp.sum(m.reshape(kc // 8, 8, qb), axis=0)
        acc = lax.fori_loop(0, nch, body, jnp.zeros((8, qb), I32))
        return jnp.sum(acc, axis=0, keepdims=True)

    one, zero = jnp.int32(1), jnp.int32(0)
    plane_chunk = lax.broadcasted_iota(I32, (planes_scr.shape[1], qb), 0) // 8
    alive = jnp.where(plane_chunk < nch, -1, 0).astype(I32)
    thr_u, need, n_eq = _kth_largest_from_planes(lambda i: planes_scr[i], alive, topk, lane_q)
    thr = thr_u ^ jnp.int32(INT_MIN)
    excess = jnp.where(thr == INT_MIN, 0, n_eq - need)
    take_all = jnp.where(thr == INT_MIN, 0, seq_len + 1).astype(I32)

    def tie_break():
        below = lambda jc: count(lambda k, p: jnp.where(k == thr, jnp.where(p < jc, one, zero), zero))
        cut = _tie_cut(below, need, max(1, int(math.ceil(math.log2(seq_len)))), lane_q)
        return jnp.where(excess > 0, cut, take_all)

    cut = lax.cond(jnp.max(excess) > 0, tie_break, lambda: take_all)

    m_scr[...] = jnp.full(m_scr.shape, NEG, F32)
    acc_scr[...] = jnp.zeros(acc_scr.shape, F32)
    sub = ATT_SUB_KEYS
    ones_rows = jnp.ones((ONES_ROWS, sub), BF16)
    sub_pos = lax.broadcasted_iota(I32, (sub, qb), 0)

    def attend(c, carry):
        off = pl.multiple_of(c * kc, kc)
        steps = [(s, hd) for s in range(kc // sub) for hd in range(ATT_HEADS)]
        biases = {}

        def logits(s, hd):
            if s not in biases:
                k = keys_scr[c, s * sub:(s + 1) * sub, :]
                pos = sub_pos + (c * kc + s * sub)
                biases[s] = jnp.where(k > thr, 0.0, jnp.where(k == thr, jnp.where(pos < cut, 0.0, NEG), NEG))
            col = slice(LANES * (hd // 2), LANES * (hd // 2 + 1))
            kcol = ak_ref[0, pl.ds(off + s * sub, sub), col]
            return lax.dot_general(kcol, aqm_scr[hd], _NT, preferred_element_type=F32) + biases[s]

        ahead = [logits(*steps[i]) for i in range(ATT_LOOKAHEAD)]
        for i, (s, hd) in enumerate(steps):
            lg = ahead.pop(0)
            if i + ATT_LOOKAHEAD < len(steps):
                ahead.append(logits(*steps[i + ATT_LOOKAHEAD]))
            m_old = m_scr[hd]
            m_new = jnp.maximum(m_old, jnp.max(lg, axis=0, keepdims=True))
            m_scr[hd] = m_new
            p = jnp.exp2(lg - m_new).astype(BF16)
            vt = jnp.concatenate([avt_ref[0, c * (kc // sub) + s, ATT_DH * hd:ATT_DH * (hd + 1), :], ones_rows],
                                 axis=0)
            acc_scr[hd] = jnp.exp2(m_old - m_new) * acc_scr[hd] + jnp.dot(vt, p, preferred_element_type=F32)
        return carry

    lax.fori_loop(0, nch, attend, 0)
    for hd in range(ATT_HEADS):
        acc = acc_scr[hd]
        outt_scr[ATT_DH * hd:ATT_DH * (hd + 1), :] = acc[:ATT_DH] / acc[ATT_DH:ATT_DH + 1]
    o_ref[0] = outt_scr[...].T.astype(BF16)


def _dsa_prompt_call(iq, iwt, aq, ik2, ak, avt):
    bg, tg, _ = iq.shape
    qb = KEY_CHUNK
    nc = tg // qb
    topk = min(TOPK_MAX, tg // 4)
    assert topk <= qb
    blk = pl.BlockSpec((1, qb, HALF), lambda b, j: (b, j, 0))
    return pl.pallas_call(
        functools.partial(_dsa_prompt_kernel, topk=topk, seq_len=tg),
        grid=(bg, nc),
        in_specs=[blk,
                  pl.BlockSpec((1, IDX_HEADS, qb), lambda b, j: (b, 0, j)),
                  blk,
                  pl.BlockSpec((1, tg, LANES), lambda b, j: (b, 0, 0)),
                  pl.BlockSpec((1, tg, HALF), lambda b, j: (b, 0, 0)),
                  pl.BlockSpec((1, tg // ATT_SUB_KEYS, HALF, ATT_SUB_KEYS), lambda b, j: (b, 0, 0, 0))],
        out_specs=blk,
        out_shape=jax.ShapeDtypeStruct((bg, tg, HALF), BF16),
        scratch_shapes=[pltpu.VMEM((nc, qb, qb), I32), pltpu.VMEM((32, nc * qb // 32, qb), I32),
                        pltpu.VMEM((ATT_HEADS, qb, LANES), BF16), pltpu.VMEM((ATT_HEADS, qb, LANES), BF16),
                        pltpu.VMEM((ATT_HEADS, 1, qb), F32), pltpu.VMEM((ATT_HEADS, ATT_DH + ONES_ROWS, qb), F32),
                        pltpu.VMEM((HALF, qb), F32)],
        compiler_params=_params(("parallel", "arbitrary")),
        name="dsa_prompt",
    )(iq, iwt, aq, ik2, ak, avt)


def _dsa_sample_kernel(iq_ref, iw_ref, aq_ref, ik2n_ref, akn_ref, avn_ref, cikt_ref, ckt_ref, cvt_ref, o_ref,
                       planes_scr, *, topk, past_len):
    nq = iq_ref.shape[1]
    n_new = nq
    total = past_len + n_new
    rows = ATT_HEADS * nq
    assert rows == 2 * LANES and past_len % 256 == 0 and n_new % 8 == 0 and n_new <= LANES
    iq = iq_ref[0]
    aq = aq_ref[0]

    iq_rows = jnp.concatenate([iq[:, ATT_DH * hd:ATT_DH * (hd + 1)] for hd in range(IDX_HEADS)], axis=0)
    iw_t = iw_ref[0].T
    iw_lane = jnp.concatenate([iw_t[hd:hd + 1, :] for hd in range(IDX_HEADS)], axis=1)

    def index_scores(ik):
        s = lax.dot_general(ik.astype(BF16), iq_rows, _NT, preferred_element_type=F32)
        w = jnp.maximum(s, 0.0) * iw_lane
        r = w[:, :LANES] + w[:, LANES:]
        r = r + pltpu.roll(r, 2 * nq, 1)
        return r + pltpu.roll(r, nq, 1)

    lane = lax.broadcasted_iota(I32, (1, LANES), 1)
    qpos = past_len + lane % nq
    limit = (qpos // MASK_CHUNK + 1) * MASK_CHUNK
    pos_c = lax.broadcasted_iota(I32, (past_len, LANES), 0)
    row_n = lax.broadcasted_iota(I32, (LANES, LANES), 0)
    pos_n = past_len + row_n
    pad_rows = lambda a: jnp.concatenate([a, jnp.zeros((LANES - n_new, a.shape[1]), a.dtype)], axis=0)
    ik_n = pad_rows(ik2n_ref[0][:, :ATT_DH])
    key_c = jnp.where(pos_c < limit, _sortable_key(index_scores(cikt_ref[0].T)), INT_MIN)
    key_n = jnp.where(row_n < n_new, jnp.where(pos_n < limit, _sortable_key(index_scores(ik_n)), INT_MIN), INT_MIN)

    n_groups = past_len // 256
    for g in range(n_groups + 1):
        if g < n_groups:
            words = [key_c[256 * g + 8 * r:256 * g + 8 * (r + 1), :] ^ jnp.int32(INT_MIN) for r in range(32)]
        else:
            words = [key_n[8 * r:8 * (r + 1), :] ^ jnp.int32(INT_MIN) if 8 * r < LANES
                     else jnp.zeros((8, LANES), I32) for r in range(32)]
        planes = _bit_planes(words)
        for i in range(32):
            planes_scr[i, 8 * g:8 * (g + 1), :] = planes[i]

    alive = jnp.full((planes_scr.shape[1], LANES), -1, I32)
    thr_u, need, n_eq = _kth_largest_from_planes(lambda i: planes_scr[i], alive, topk, lane)
    thr = thr_u ^ jnp.int32(INT_MIN)
    excess = jnp.where(thr == INT_MIN, 0, n_eq - need)
    take_all = jnp.where(thr == INT_MIN, 0, total + 1).astype(I32)

    one, zero = jnp.int32(1), jnp.int32(0)

    def tie_break():
        def below(jc):
            pred = lambda k, p: jnp.where(k == thr, jnp.where(p < jc, one, zero), zero)
            return (jnp.sum(pred(key_c, pos_c), axis=0, keepdims=True)
                    + jnp.sum(pred(key_n, pos_n), axis=0, keepdims=True))
        cut = _tie_cut(below, need, max(1, int(math.ceil(math.log2(total)))), lane)
        return jnp.where(excess > 0, cut, take_all)

    cut = lax.cond(jnp.max(excess) > 0, tie_break, lambda: take_all)

    def bias(k, pos):
        b = jnp.where(k > thr, 0.0, jnp.where(k == thr, jnp.where(pos < cut, 0.0, NEG), NEG))
        return b.T[:nq]

    lane_head = lax.broadcasted_iota(I32, (nq, HALF), 1) // ATT_DH
    q_bd = jnp.concatenate([jnp.where(lane_head == hd, aq, jnp.zeros((), BF16)) for hd in range(ATT_HEADS)], axis=0)

    def masked(lg, b):
        return (lg.reshape(ATT_HEADS, nq, lg.shape[1]) + b[None]).reshape(rows, lg.shape[1])

    k_n = pad_rows(akn_ref[0]).astype(BF16)
    v_n = pad_rows(avn_ref[0]).astype(BF16)
    lg_c = masked(jnp.dot(q_bd, ckt_ref[0].astype(BF16), preferred_element_type=F32), bias(key_c, pos_c))
    lg_n = masked(lax.dot_general(q_bd, k_n, _NT, preferred_element_type=F32), bias(key_n, pos_n))
    m = jnp.maximum(jnp.max(lg_c, axis=1, keepdims=True), jnp.max(lg_n, axis=1, keepdims=True))
    p_c = jnp.exp2(lg_c - m).astype(BF16)
    p_n = jnp.exp2(lg_n - m).astype(BF16)
    denom = (jnp.dot(p_c, jnp.ones((past_len, LANES), BF16), preferred_element_type=F32)
             + jnp.dot(p_n, jnp.ones((LANES, LANES), BF16), preferred_element_type=F32))
    res = (lax.dot_general(p_c, cvt_ref[0].astype(BF16), _NT, preferred_element_type=F32)
           + jnp.dot(p_n, v_n, preferred_element_type=F32))
    inv = 1.0 / denom
    res = (res * jnp.concatenate([inv] * (HALF // LANES), axis=1)).reshape(ATT_HEADS, nq, HALF)
    out = jnp.zeros((nq, HALF), F32)
    for hd in range(ATT_HEADS):
        out = out + jnp.where(lane_head == hd, res[hd], 0.0)
    o_ref[0] = out.astype(BF16)


def _dsa_sample_call(iq, iw, aq, ik2, ak, av, cache_ikt, cache_kt, cache_vt):
    bg, nq, _ = iq.shape
    past_len = cache_ikt.shape[2]
    assert cache_kt.shape == cache_vt.shape == (bg, HALF, past_len)
    topk = min(TOPK_MAX, (past_len + nq) // 4)
    per_b = lambda n, w: pl.BlockSpec((1, n, w), lambda b: (b, 0, 0))
    return pl.pallas_call(
        functools.partial(_dsa_sample_kernel, topk=topk, past_len=past_len),
        grid=(bg,),
        in_specs=[per_b(nq, HALF), per_b(nq, LANES), per_b(nq, HALF), per_b(nq, LANES), per_b(nq, HALF),
                  per_b(nq, HALF), per_b(ATT_DH, past_len), per_b(HALF, past_len), per_b(HALF, past_len)],
        out_specs=per_b(nq, HALF),
        out_shape=jax.ShapeDtypeStruct((bg, nq, HALF), BF16),
        scratch_shapes=[pltpu.VMEM((32, 8 * (past_len // 256 + 1), LANES), I32)],
        compiler_params=_params(("parallel",)),
        name="dsa_sample",
    )(iq, iw, aq, ik2, ak, av, cache_ikt, cache_kt, cache_vt)


def _pad_w_in(w_in):
    d = w_in.shape[0]
    main = 8 * HALF
    ik = w_in[:, main:main + ATT_DH]
    iw = w_in[:, main + ATT_DH:]
    pad = jnp.zeros((d, LANES - iw.shape[1]), w_in.dtype)
    return jnp.concatenate([w_in[:, :main], ik, ik, iw, pad], axis=1).astype(BF16)


def _layer(x_p, x_s, c_p, c_s, past_k, past_v, past_ik, ret_s0, depth,
           w_cond, b_cond, f1g, f1u, f1d, ln1_g, ln1_b, w_in, w_out, ln2_g, ln2_b, f2g, f2u, f2d, ln3_g, ln3_b):
    alpha = (2.0 * depth) ** 0.25
    bp, tp, d = x_p.shape
    bs, ts, _ = x_s.shape
    past_len = past_k.shape[1]

    mod = _cond_call(jnp.concatenate([c_p, c_s], axis=0), w_cond, b_cond).reshape(bp + bs, 9, d)
    mod_p, mod_s = mod[:bp], mod[bp:]

    w_in_p = _pad_w_in(w_in)
    w_out_b = w_out.astype(BF16)
    f1 = (f1g.astype(BF16), f1u.astype(BF16), f1d.astype(BF16), ln1_g, ln1_b)
    f2 = (f2g.astype(BF16), f2u.astype(BF16), f2d.astype(BF16), ln3_g, ln3_b)

    tt_p = min(512, tp)
    bb_s = min(bs, max(1, 512 // ts))
    geo_p = dict(bb=1, tt=tt_p)
    geo_s = dict(bb=bb_s, tt=ts)

    x1_p = _block_call(x_p, mod_p, *f1, alpha=alpha, mod_idx=(0, 1, 2), **geo_p)
    x1_s = _block_call(x_s, mod_s, *f1, alpha=alpha, mod_idx=(0, 1, 2), **geo_s)

    tabs_p = _rope_tables(jnp.arange(tp, dtype=I32))
    tabs_s = _rope_tables(past_len + jnp.arange(ts, dtype=I32))
    (rq, rk, rv, rg, aq, iq, ak_p, ik2_p, iwt, akt_p, avt_p, ikt_p, avc) = _mixer_in_call(
        x1_p, mod_p, w_in_p, tabs_p, bb=1, tt=KEY_CHUNK, prompt_layout=True)
    ret_chunk = min(256, tp)
    oret_p, state_p = _retention_call(rq, rk, rv, rg, jnp.zeros((bp, RET_HEADS, RET_DK, RET_DK), F32),
                                      chunk=ret_chunk)
    oatt_p = _dsa_prompt_call(iq, iwt, aq, ik2_p, ak_p, avc)

    (rq, rk, rv, rg, aq, iq, ak_s, ik2_s, iw, av_s) = _mixer_in_call(
        x1_s, mod_s, w_in_p, tabs_s, prompt_layout=False, **geo_s)
    oret_s, state_s = _retention_call(rq, rk, rv, rg, ret_s0, chunk=ts)
    feat_major = lambda a: a.transpose(0, 2, 3, 1).reshape(bs, HALF, past_len)
    oatt_s = _dsa_sample_call(iq, iw, aq, ik2_s, ak_s, av_s, past_ik.transpose(0, 2, 1),
                              feat_major(past_k), feat_major(past_v))

    y_p = _block_call(x1_p, mod_p, *f2, alpha=alpha, mod_idx=(6, 7, 8), mix=(oret_p, oatt_p, w_out_b, ln2_g, ln2_b),
                      **geo_p)
    y_s = _block_call(x1_s, mod_s, *f2, alpha=alpha, mod_idx=(6, 7, 8), mix=(oret_s, oatt_s, w_out_b, ln2_g, ln2_b),
                      **geo_s)

    heads = lambda a: a.reshape(a.shape[0], a.shape[1], ATT_HEADS, ATT_DH)
    heads_t = lambda a: a.reshape(bp, ATT_HEADS, ATT_DH, tp).transpose(0, 3, 1, 2)
    st_p = (heads_t(akt_p), heads_t(avt_p), ikt_p.transpose(0, 2, 1), state_p)
    st_s = (heads(ak_s), heads(av_s), ik2_s[..., :ATT_DH], state_s)
    return y_p, y_s, st_p, st_s


def kernel(x_prompt, x_sample, c_prompt, c_sample, cache_k, cache_v, cache_idx_k, state_ret, w_cond, b_cond, ffn1_w_gate, ffn1_w_up, ffn1_w_down, ln1_g, ln1_b, w_in, w_out, ln2_g, ln2_b, ffn2_w_gate, ffn2_w_up, ffn2_w_down, ln3_g, ln3_b):
    depth = w_cond.shape[0]
    y_p, y_s = x_prompt, x_sample
    st_p, st_s = [], []
    for l in range(depth):
        y_p, y_s, sp, ss = _layer(
            y_p, y_s, c_prompt, c_sample, cache_k[l], cache_v[l], cache_idx_k[l], state_ret[l], depth,
            w_cond[l], b_cond[l], ffn1_w_gate[l], ffn1_w_up[l], ffn1_w_down[l], ln1_g[l], ln1_b[l],
            w_in[l], w_out[l], ln2_g[l], ln2_b[l], ffn2_w_gate[l], ffn2_w_up[l], ffn2_w_down[l], ln3_g[l], ln3_b[l])
        st_p.append(sp)
        st_s.append(ss)
    stack = lambda sts, i: jnp.stack([s[i] for s in sts])
    return (y_p, y_s, stack(st_p, 0), stack(st_p, 1), stack(st_p, 2), stack(st_p, 3),
            stack(st_s, 0), stack(st_s, 1), stack(st_s, 2), stack(st_s, 3))
```
